```python
import math
import jax, jax.numpy as jnp
from jax import lax
import numpy as np

D_MODEL = 2048
BATCH = 8
SEQ = 4096
DEPTH = 4

CHUNK = 64
Q_BLOCK = 128
N_A_LAYERS = DEPTH // 2
N_B_LAYERS = DEPTH - N_A_LAYERS

SB_HEADS = 16
SB_HEAD_DIM = D_MODEL // SB_HEADS

MLA_HEADS = 16
QK_NOPE_DIM = 128
QK_ROPE_DIM = 64
V_HEAD_DIM = 128
Q_LORA_RANK = 512
KV_LORA_RANK = 512
ROPE_BASE = 10000.0

N_EXPERTS = 32
TOP_K = 4
D_FF_EXPERT = 768
SWIGLU_ALPHA = 1.702
SWIGLU_LIMIT = 7.0
EXPERT_ROW_BLOCK = 128

RMS_EPS = 1e-6

kernel_name = "yoco_stickbreak_mla_moe_adaln"


def rmsnorm(x, g):
    xf = x.astype(jnp.float32)
    y = xf * lax.rsqrt(jnp.mean(xf * xf, axis=-1, keepdims=True) + RMS_EPS)
    return (y * g.astype(jnp.float32)).astype(x.dtype)


def adaln(c, w, b, n):
    mod = jnp.einsum('bd,de->be', jax.nn.silu(c), w) + b
    return jnp.split(mod, n, axis=-1)


def modulate(h, shift, scale):
    return h * (1.0 + scale[:, None, :]) + shift[:, None, :]


def rope_tables(positions):
    half = QK_ROPE_DIM // 2
    inv_freq = ROPE_BASE ** (-jnp.arange(half, dtype=jnp.float32) / half)
    ang = positions.astype(jnp.float32)[..., None] * inv_freq
    return jnp.cos(ang), jnp.sin(ang)


def apply_rope(x, cos, sin):
    xf = x.astype(jnp.float32)
    x1, x2 = jnp.split(xf, 2, axis=-1)
    return jnp.concatenate([x1 * cos - x2 * sin, x2 * cos + x1 * sin], axis=-1).astype(x.dtype)


def stick_breaking_attention(q, k, v):
    S = q.shape[1]
    scale = q.shape[-1] ** -0.5
    outs = []
    for t0 in range(0, S, Q_BLOCK):
        t1 = t0 + Q_BLOCK
        z = jnp.einsum('bqhd,bkhd->bhqk', q[:, t0:t1], k[:, :t1],
                       preferred_element_type=jnp.float32) * scale
        strict = jnp.arange(t1)[None, :] < jnp.arange(t0, t1)[:, None]
        log_keep = jnp.where(strict, jax.nn.log_sigmoid(-z), 0.0)
        log_a = jax.nn.log_sigmoid(z) + lax.cumsum(log_keep, axis=3, reverse=True) - log_keep
        a = jnp.where(strict, jnp.exp(log_a), 0.0).astype(v.dtype)
        outs.append(jnp.einsum('bhqk,bkhd->bqhd', a, v[:, :t1]))
    return jnp.concatenate(outs, axis=1)


def latent_attention(q_nope, q_rope, k_nope, k_rope, v):
    S = q_nope.shape[1]
    scale = (QK_NOPE_DIM + QK_ROPE_DIM) ** -0.5
    outs = []
    for t0 in range(0, S, Q_BLOCK):
        t1 = t0 + Q_BLOCK
        s = (jnp.einsum('bqhd,bkhd->bhqk', q_nope[:, t0:t1], k_nope[:, :t1],
                        preferred_element_type=jnp.float32)
             + jnp.einsum('bqhd,bkd->bhqk', q_rope[:, t0:t1], k_rope[:, :t1],
                          preferred_element_type=jnp.float32)) * scale
        allowed = (jnp.arange(t1)[None, :] // CHUNK) <= (jnp.arange(t0, t1)[:, None] // CHUNK)
        p = jax.nn.softmax(jnp.where(allowed, s, -jnp.inf), axis=-1).astype(v.dtype)
        outs.append(jnp.einsum('bhqk,bkhd->bqhd', p, v[:, :t1]))
    return jnp.concatenate(outs, axis=1)


def shared_latent_kv(x, c, cos, sin, g_kv_in, w_mod_kv, b_mod_kv, w_dkv, g_ckv, w_uk, w_uv):
    B, S, _ = x.shape
    shift, scale = adaln(c, w_mod_kv, b_mod_kv, 2)
    h = modulate(rmsnorm(x, g_kv_in), shift, scale)
    lat = h @ w_dkv
    c_kv = rmsnorm(lat[..., :KV_LORA_RANK], g_ckv)
    k_rope = apply_rope(lat[..., KV_LORA_RANK:], cos, sin)
    k_nope = (c_kv @ w_uk).reshape(B, S, MLA_HEADS, QK_NOPE_DIM)
    v = (c_kv @ w_uv).reshape(B, S, MLA_HEADS, V_HEAD_DIM)
    return k_nope, k_rope, v


def moe_clamped_swiglu(h, router_w, router_b, w_gu, b_gu, w_down, b_down):
    B, S, D = h.shape
    xt = h.reshape(-1, D)
    T = xt.shape[0]
    logits = (xt @ router_w + router_b).astype(jnp.float32)
    top_val, top_idx = lax.top_k(logits, TOP_K)
    gates = jax.nn.softmax(top_val, axis=-1).astype(h.dtype)

    M = T * TOP_K
    e_flat = top_idx.reshape(-1).astype(jnp.int32)
    tok_flat = jnp.arange(M, dtype=jnp.int32) // TOP_K
    order = jnp.argsort(e_flat)
    e_sorted = e_flat[order]
    counts = jnp.bincount(e_flat, length=N_EXPERTS).astype(jnp.int32)
    starts = jnp.cumsum(counts) - counts
    padded = (counts + EXPERT_ROW_BLOCK - 1) // EXPERT_ROW_BLOCK * EXPERT_ROW_BLOCK
    pends = jnp.cumsum(padded)
    pstarts = pends - padded
    dest = pstarts[e_sorted] + (jnp.arange(M, dtype=jnp.int32) - starts[e_sorted])

    n_rows = M + N_EXPERTS * EXPERT_ROW_BLOCK
    n_blocks = n_rows // EXPERT_ROW_BLOCK
    row_tok = jnp.zeros((n_rows,), jnp.int32).at[dest].set(tok_flat[order])
    row_gate = jnp.zeros((n_rows,), h.dtype).at[dest].set(gates.reshape(-1)[order])
    block_start = jnp.arange(n_blocks, dtype=jnp.int32) * EXPERT_ROW_BLOCK
    block_expert = jnp.minimum(jnp.searchsorted(pends, block_start, side='right'),
                               N_EXPERTS - 1).astype(jnp.int32)

    def expert_block(args):
        tok, gate, e = args
        xb = xt[tok]
        gu = xb @ w_gu[e] + b_gu[e]
        g, u = jnp.split(gu, 2, axis=-1)
        g = jnp.minimum(g, SWIGLU_LIMIT)
        u = jnp.clip(u, -SWIGLU_LIMIT, SWIGLU_LIMIT)
        a = g * jax.nn.sigmoid(SWIGLU_ALPHA * g) * (u + 1.0)
        return (a @ w_down[e] + b_down[e]) * gate[:, None]

    y = lax.map(expert_block, (row_tok.reshape(n_blocks, EXPERT_ROW_BLOCK),
                               row_gate.reshape(n_blocks, EXPERT_ROW_BLOCK),
                               block_expert))
    out = jnp.zeros_like(xt).at[row_tok].add(y.reshape(n_rows, D))
    return out.reshape(B, S, D)


def setup_inputs(seed: int = 0) -> dict:
    key = jax.random.key(seed)
    ks = jax.random.split(key, 32)
    D, H = D_MODEL, MLA_HEADS
    nrm = lambda k, shape, s: jax.random.normal(k, shape, jnp.float32) * s
    offsets = jax.random.randint(ks[2], (BATCH, 1), 0, 4096)
    positions = (offsets + jnp.arange(SEQ)[None, :]).astype(jnp.int32)
    return {
        "x": nrm(ks[0], (BATCH, SEQ, D), 1.0),
        "c": nrm(ks[1], (BATCH, D), 1.0),
        "positions": positions,
        "w_mod": nrm(ks[3], (DEPTH, D, 6 * D), 0.5 * D ** -0.5),
        "b_mod": nrm(ks[4], (DEPTH, 6 * D), 0.02),
        "g_mix": 1.0 + nrm(ks[5], (DEPTH, D), 0.02),
        "g_ffn": 1.0 + nrm(ks[6], (DEPTH, D), 0.02),
        "w_qkv_a": nrm(ks[7], (N_A_LAYERS, D, 3 * SB_HEADS * SB_HEAD_DIM), D ** -0.5),
        "w_o_a": nrm(ks[8], (N_A_LAYERS, SB_HEADS * SB_HEAD_DIM, D), (SB_HEADS * SB_HEAD_DIM) ** -0.5),
        "g_kv_in": 1.0 + nrm(ks[9], (D,), 0.02),
        "w_mod_kv": nrm(ks[10], (D, 2 * D), 0.5 * D ** -0.5),
        "b_mod_kv": nrm(ks[11], (2 * D,), 0.02),
        "w_dkv": nrm(ks[12], (D, KV_LORA_RANK + QK_ROPE_DIM), D ** -0.5),
        "g_ckv": 1.0 + nrm(ks[13], (KV_LORA_RANK,), 0.02),
        "w_uk": nrm(ks[14], (KV_LORA_RANK, H * QK_NOPE_DIM), KV_LORA_RANK ** -0.5),
        "w_uv": nrm(ks[15], (KV_LORA_RANK, H * V_HEAD_DIM), KV_LORA_RANK ** -0.5),
        "w_dq_b": nrm(ks[16], (N_B_LAYERS, D, Q_LORA_RANK), D ** -0.5),
        "g_cq_b": 1.0 + nrm(ks[17], (N_B_LAYERS, Q_LORA_RANK), 0.02),
        "w_uq_b": nrm(ks[18], (N_B_LAYERS, Q_LORA_RANK, H * (QK_NOPE_DIM + QK_ROPE_DIM)), Q_LORA_RANK ** -0.5),
        "w_o_b": nrm(ks[19], (N_B_LAYERS, H * V_HEAD_DIM, D), (H * V_HEAD_DIM) ** -0.5),
        "router_w": nrm(ks[20], (DEPTH, D, N_EXPERTS), D ** -0.5),
        "router_b": nrm(ks[21], (DEPTH, N_EXPERTS), 0.01),
        "w_gu": nrm(ks[22], (DEPTH, N_EXPERTS, D, 2 * D_FF_EXPERT), D ** -0.5),
        "b_gu": nrm(ks[23], (DEPTH, N_EXPERTS, 2 * D_FF_EXPERT), 0.02),
        "w_down": nrm(ks[24], (DEPTH, N_EXPERTS, D_FF_EXPERT, D), D_FF_EXPERT ** -0.5),
        "b_down": nrm(ks[25], (DEPTH, N_EXPERTS, D), 0.02),
        "g_final": 1.0 + nrm(ks[26], (D,), 0.02),
    }


def reference(x, c, positions, w_mod, b_mod, g_mix, g_ffn, w_qkv_a, w_o_a, g_kv_in, w_mod_kv,
              b_mod_kv, w_dkv, g_ckv, w_uk, w_uv, w_dq_b, g_cq_b, w_uq_b, w_o_b, router_w,
              router_b, w_gu, b_gu, w_down, b_down, g_final):
    B, S, D = x.shape
    cos, sin = rope_tables(positions)
    shared = None
    for layer in range(DEPTH):
        shift_a, scale_a, gate_a, shift_m, scale_m, gate_m = adaln(c, w_mod[layer], b_mod[layer], 6)
        h = modulate(rmsnorm(x, g_mix[layer]), shift_a, scale_a)
        if layer < N_A_LAYERS:
            qkv = (h @ w_qkv_a[layer]).reshape(B, S, 3, SB_HEADS, SB_HEAD_DIM)
            o = stick_breaking_attention(qkv[:, :, 0], qkv[:, :, 1], qkv[:, :, 2])
            mix = o.reshape(B, S, SB_HEADS * SB_HEAD_DIM) @ w_o_a[layer]
        else:
            i = layer - N_A_LAYERS
            k_nope, k_rope, v_b = shared
            cq = rmsnorm(h @ w_dq_b[i], g_cq_b[i])
            qf = (cq @ w_uq_b[i]).reshape(B, S, MLA_HEADS, QK_NOPE_DIM + QK_ROPE_DIM)
            q_nope = qf[..., :QK_NOPE_DIM]
            q_rope = apply_rope(qf[..., QK_NOPE_DIM:], cos[:, :, None, :], sin[:, :, None, :])
            o = latent_attention(q_nope, q_rope, k_nope, k_rope, v_b)
            mix = o.reshape(B, S, MLA_HEADS * V_HEAD_DIM) @ w_o_b[i]
        x = x + gate_a[:, None, :] * mix
        h = modulate(rmsnorm(x, g_ffn[layer]), shift_m, scale_m)
        x = x + gate_m[:, None, :] * moe_clamped_swiglu(h, router_w[layer], router_b[layer], w_gu[layer],
                                                        b_gu[layer], w_down[layer], b_down[layer])
        if layer == N_A_LAYERS - 1:
            shared = shared_latent_kv(x, c, cos, sin, g_kv_in, w_mod_kv, b_mod_kv, w_dkv, g_ckv, w_uk, w_uv)
    return rmsnorm(x, g_final)
```

```python
import functools

import jax
import jax.numpy as jnp
from jax import lax
from jax.experimental import pallas as pl
from jax.experimental.pallas import tpu as pltpu

F32 = jnp.float32
BF16 = jnp.bfloat16
I32 = jnp.int32

CHUNK = 64
SB_HEADS = 16
SB_HEAD_DIM = 128
MLA_HEADS = 16
QK_NOPE_DIM = 128
QK_ROPE_DIM = 64
V_HEAD_DIM = 128
KV_LORA_RANK = 512
ROPE_BASE = 10000.0
N_EXPERTS = 32
TOP_K = 4
D_FF_EXPERT = 768
SWIGLU_ALPHA = 1.702
SWIGLU_LIMIT = 7.0
RMS_EPS = 1e-6

V7X_VMEM_LIMIT_BYTES = 52 * 1024 * 1024
LANES = 128
QPAD = 256

NEG_INF = float("-inf")


def _cparams(*sem):
    return pltpu.CompilerParams(dimension_semantics=sem, vmem_limit_bytes=V7X_VMEM_LIMIT_BYTES)


def _split_bf16(a):
    hi = a.astype(BF16)
    lo = (a - hi.astype(F32)).astype(BF16)
    return hi, lo


def _dot(a, b):
    return jnp.dot(a, b, preferred_element_type=F32)


def _dot_nt(a, b):
    return lax.dot_general(a, b, (((1,), (1,)), ((), ())), preferred_element_type=F32)


def _rms(x, g):
    ms = jnp.mean(x * x, axis=-1, keepdims=True)
    return x * lax.rsqrt(ms + RMS_EPS) * g


def _norm_mod(x, g, shift, scale):
    return _rms(x, g) * (1.0 + scale) + shift


def _adaln_kernel(c_ref, w_ref, b_ref, o_ref):
    c = c_ref[...]
    s = c * (1.0 / (1.0 + jnp.exp(-c)))
    s_hi, s_lo = _split_bf16(s)
    w_hi, w_lo = _split_bf16(w_ref[0])
    o_ref[0] = _dot(s_hi, w_hi) + _dot(s_lo, w_hi) + _dot(s_hi, w_lo) + b_ref[0]


def _adaln(c, w, b):
    L, D, N = w.shape
    B = c.shape[0]
    tn = 1024
    return pl.pallas_call(
        _adaln_kernel,
        grid=(L, N // tn),
        in_specs=[pl.BlockSpec((B, D), lambda l, j: (0, 0)),
                  pl.BlockSpec((1, D, tn), lambda l, j: (l, 0, j)),
                  pl.BlockSpec((1, 1, tn), lambda l, j: (l, 0, j))],
        out_specs=pl.BlockSpec((1, B, tn), lambda l, j: (l, 0, j)),
        out_shape=jax.ShapeDtypeStruct((L, B, N), F32),
        compiler_params=_cparams("parallel", "parallel"),
        name="adaln",
    )(c, w, b.reshape(L, 1, N))


def _cossin_kernel(a_ref, c_ref, s_ref):
    a = a_ref[...]
    c_ref[...] = jnp.cos(a)
    s_ref[...] = jnp.sin(a)


def _cossin(ang):
    R = ang.shape[0]
    tr = min(R, 512)
    spec = pl.BlockSpec((tr, LANES), lambda i: (i, 0))
    return pl.pallas_call(
        _cossin_kernel, grid=(R // tr,), in_specs=[spec], out_specs=[spec, spec],
        out_shape=[jax.ShapeDtypeStruct(ang.shape, F32)] * 2,
        compiler_params=_cparams("parallel"), name="cossin",
    )(ang)


def _nm_matmul_kernel(x_ref, g_ref, sh_ref, sc_ref, w_ref, o_ref, h_ref, *, n_scaled, col_scale):
    j = pl.program_id(1)

    @pl.when(j == 0)
    def _():
        h_ref[...] = _norm_mod(x_ref[...], g_ref[...], sh_ref[0], sc_ref[0]).astype(BF16)

    acc = _dot(h_ref[...], w_ref[...])
    s = jnp.where(j < n_scaled, col_scale, 1.0).astype(F32)
    o_ref[...] = (acc * s).astype(o_ref.dtype)


def _nm_matmul(x, g, shift, scale, w, S, *, q_cols=0, q_scale=1.0):
    T, D = x.shape
    N = w.shape[1]
    tm = min(S, 512)
    tn = 1024
    per_b = S // tm
    B = T // S
    mod_spec = pl.BlockSpec((1, 1, D), lambda i, j: (i // per_b, 0, 0))
    return pl.pallas_call(
        functools.partial(_nm_matmul_kernel, n_scaled=q_cols // tn, col_scale=q_scale),
        grid=(T // tm, N // tn),
        in_specs=[pl.BlockSpec((tm, D), lambda i, j: (i, 0)),
                  pl.BlockSpec((1, D), lambda i, j: (0, 0)),
                  mod_spec, mod_spec,
                  pl.BlockSpec((D, tn), lambda i, j: (0, j))],
        out_specs=pl.BlockSpec((tm, tn), lambda i, j: (i, j)),
        out_shape=jax.ShapeDtypeStruct((T, N), BF16),
        scratch_shapes=[pltpu.VMEM((tm, D), BF16)],
        compiler_params=_cparams("parallel", "arbitrary"),
        name="norm_mod_matmul",
    )(x, g.reshape(1, D), shift.reshape(B, 1, D), scale.reshape(B, 1, D), w)


def _sb_attn_kernel(q_ref, k_ref, v_ref, tri_ref, o_ref, *, tq, tk):
    qi = pl.program_id(2)
    q = q_ref[...]
    tri = tri_ref[...]
    n_kb = ((qi + 1) * tq) // tk
    t_idx = qi * tq + lax.broadcasted_iota(I32, (tq, tk), 0)
    s_loc = lax.broadcasted_iota(I32, (tq, tk), 1)

    def body(jj, carry):
        acc, run = carry
        ks = pl.multiple_of((n_kb - 1 - jj) * tk, tk)
        kb = k_ref[pl.ds(ks, tk), :]
        vb = v_ref[pl.ds(ks, tk), :]
        z = _dot_nt(q, kb)
        mask = (ks + s_loc) < t_idx
        sp = jnp.maximum(z, 0.0) + jnp.log(1.0 + jnp.exp(-jnp.abs(z)))
        lk = jnp.where(mask, -sp, 0.0)
        lk_hi, lk_lo = _split_bf16(lk)
        p = _dot(lk_hi, tri) + _dot(lk_lo, tri)
        log_a = z + p[:, :tk] + run
        a = jnp.where(mask, jnp.exp(log_a), 0.0).astype(BF16)
        acc = acc + _dot(a, vb)
        run = run + p[:, tk:]
        return acc, run

    zero = jnp.zeros((tq, LANES), F32)
    acc, _ = lax.fori_loop(0, n_kb, body, (zero, zero))
    o_ref[...] = acc.astype(o_ref.dtype)


def _sb_attention(qkv, B, S):
    H, Dh = SB_HEADS, SB_HEAD_DIM
    T = qkv.shape[0]
    tq = tk = 128
    nq = S // tq
    jj = jnp.arange(tk)[:, None]
    ss = jnp.arange(tk + LANES)[None, :]
    tri = ((jj >= ss) | (ss >= tk)).astype(BF16)
    return pl.pallas_call(
        functools.partial(_sb_attn_kernel, tq=tq, tk=tk),
        grid=(B, H, nq),
        in_specs=[pl.BlockSpec((tq, Dh), lambda b, h, i: (b * nq + i, h)),
                  pl.BlockSpec((S, Dh), lambda b, h, i: (b, H + h)),
                  pl.BlockSpec((S, Dh), lambda b, h, i: (b, 2 * H + h)),
                  pl.BlockSpec((tk, tk + LANES), lambda b, h, i: (0, 0))],
        out_specs=pl.BlockSpec((tq, Dh), lambda b, h, i: (b * nq + i, h)),
        out_shape=jax.ShapeDtypeStruct((T, H * Dh), BF16),
        compiler_params=_cparams("parallel", "parallel", "arbitrary"),
        name="sb_attention",
    )(qkv, qkv, qkv, tri)


def _oproj_kernel(o_ref, w_ref, x_ref, ga_ref, g_ref, sh_ref, sc_ref, rwt_ref, rb_ref,
                  xo_ref, h_ref, lg_ref):
    mix = _dot(o_ref[...], w_ref[...])
    xn = x_ref[...] + ga_ref[0] * mix
    xo_ref[...] = xn
    h = _norm_mod(xn, g_ref[...], sh_ref[0], sc_ref[0])
    h_ref[...] = h
    h_hi, h_lo = _split_bf16(h)
    r_hi, r_lo = _split_bf16(rwt_ref[...])
    lg_ref[...] = _dot_nt(r_hi, h_hi) + _dot_nt(r_lo, h_hi) + _dot_nt(r_hi, h_lo) + rb_ref[...]


def _oproj(o, w, x, gate, g, shift, scale, router_w, router_b, S):
    T, D = x.shape
    E = router_w.shape[1]
    B = T // S
    tm = min(S, 256)
    per_b = S // tm
    row = pl.BlockSpec((tm, D), lambda i: (i, 0))
    mod = pl.BlockSpec((1, 1, D), lambda i: (i // per_b, 0, 0))
    return pl.pallas_call(
        _oproj_kernel,
        grid=(T // tm,),
        in_specs=[row,
                  pl.BlockSpec((D, D), lambda i: (0, 0)),
                  row, mod,
                  pl.BlockSpec((1, D), lambda i: (0, 0)),
                  mod, mod,
                  pl.BlockSpec((E, D), lambda i: (0, 0)),
                  pl.BlockSpec((E, 1), lambda i: (0, 0))],
        out_specs=[row, row, pl.BlockSpec((E, tm), lambda i: (0, i))],
        out_shape=[jax.ShapeDtypeStruct((T, D), F32),
                   jax.ShapeDtypeStruct((T, D), F32),
                   jax.ShapeDtypeStruct((E, T), F32)],
        compiler_params=_cparams("parallel"),
        name="oproj_residual_router",
    )(o, w, x, gate.reshape(B, 1, D), g.reshape(1, D), shift.reshape(B, 1, D),
      scale.reshape(B, 1, D), router_w.T, router_b.reshape(E, 1))


def _route_kernel(lg_ref, tri_ref, idx_ref, gate_ref, rank_ref, cnt_ref, carry_ref):
    i = pl.program_id(0)

    @pl.when(i == 0)
    def _():
        carry_ref[...] = jnp.zeros_like(carry_ref)

    work = lg_ref[...]
    E, tt = work.shape
    eio = lax.broadcasted_iota(I32, (E, tt), 0)
    vals, onehots = [], []
    for k in range(TOP_K):
        m = jnp.max(work, axis=0, keepdims=True)
        ix = jnp.min(jnp.where(work == m, eio, E), axis=0, keepdims=True)
        oh = eio == ix
        work = jnp.where(oh, NEG_INF, work)
        vals.append(m)
        onehots.append(oh)
        idx_ref[k:k + 1, :] = ix
    es = [jnp.exp(v - vals[0]) for v in vals]
    den = es[0] + es[1] + es[2] + es[3]
    for k in range(TOP_K):
        gate_ref[k:k + 1, :] = es[k] / den
    sel = jnp.zeros((E, tt), F32)
    for oh in onehots:
        sel = sel + jnp.where(oh, 1.0, 0.0)
    cnt = _dot(sel.astype(BF16), tri_ref[...]) + carry_ref[:, 0:1]
    for k in range(TOP_K):
        r = jnp.sum(jnp.where(onehots[k], cnt - 1.0, 0.0), axis=0, keepdims=True)
        rank_ref[k:k + 1, :] = r.astype(I32)
    last = jnp.broadcast_to(cnt[:, tt - 1:tt], (E, LANES))
    carry_ref[...] = last
    cnt_ref[...] = last


def _route(logits_t):
    E, T = logits_t.shape
    tt = min(T, 512)
    tri = (jnp.arange(tt)[:, None] <= jnp.arange(tt)[None, :]).astype(BF16)
    tok = pl.BlockSpec((TOP_K, tt), lambda i: (0, i))
    return pl.pallas_call(
        _route_kernel,
        grid=(T // tt,),
        in_specs=[pl.BlockSpec((E, tt), lambda i: (0, i)),
                  pl.BlockSpec((tt, tt), lambda i: (0, 0))],
        out_specs=[tok, tok, tok, pl.BlockSpec((E, LANES), lambda i: (0, 0))],
        out_shape=[jax.ShapeDtypeStruct((TOP_K, T), I32),
                   jax.ShapeDtypeStruct((TOP_K, T), F32),
                   jax.ShapeDtypeStruct((TOP_K, T), I32),
                   jax.ShapeDtypeStruct((E, LANES), F32)],
        scratch_shapes=[pltpu.VMEM((E, LANES), F32)],
        compiler_params=_cparams("arbitrary"),
        name="route_topk_rank",
    )(logits_t, tri)


def _dest_kernel(idx_ref, rank_ref, ps_ref, dest_ref):
    E = ps_ref.shape[0]
    tt = idx_ref.shape[1]
    eio = lax.broadcasted_iota(I32, (E, tt), 0)
    ps = ps_ref[...]
    for k in range(TOP_K):
        oh = eio == idx_ref[k:k + 1, :]
        base = jnp.sum(jnp.where(oh, ps, 0), axis=0, keepdims=True)
        dest_ref[k:k + 1, :] = base + rank_ref[k:k + 1, :]


def _dest(idx_t, rank_t, pstart):
    T = idx_t.shape[1]
    E = pstart.shape[0]
    tt = min(T, 2048)
    tok = pl.BlockSpec((TOP_K, tt), lambda i: (0, i))
    return pl.pallas_call(
        _dest_kernel, grid=(T // tt,),
        in_specs=[tok, tok, pl.BlockSpec((E, 1), lambda i: (0, 0))],
        out_specs=tok,
        out_shape=jax.ShapeDtypeStruct((TOP_K, T), I32),
        compiler_params=_cparams("parallel"), name="route_dest",
    )(idx_t, rank_t, pstart.reshape(E, 1))


def _row_copy(src, s_row, dst, d_row, sem):
    return pltpu.make_async_copy(src.at[pl.ds(s_row, 1)], dst.at[pl.ds(d_row, 1)], sem)


def _dispatch_kernel(dest_ref, h_hbm, xs_in, xs_hbm, sem, *, tt):
    del xs_in
    base = pl.program_id(0) * tt

    def issue(r, c):
        for k in range(TOP_K):
            _row_copy(h_hbm, base + r, xs_hbm, dest_ref[r * TOP_K + k], sem).start()
        return c

    lax.fori_loop(0, tt, issue, 0)

    def drain(r, c):
        for k in range(TOP_K):
            _row_copy(h_hbm, 0, xs_hbm, 0, sem).wait()
        return c

    lax.fori_loop(0, tt, drain, 0)


def _dispatch(dest_flat, h, n_rows):
    T, D = h.shape
    tt = min(T, 128)
    xs0 = jnp.zeros((n_rows, D), h.dtype)
    return pl.pallas_call(
        functools.partial(_dispatch_kernel, tt=tt),
        grid=(T // tt,),
        in_specs=[pl.BlockSpec((tt * TOP_K,), lambda i: (i,), memory_space=pltpu.SMEM),
                  pl.BlockSpec(memory_space=pl.ANY),
                  pl.BlockSpec(memory_space=pl.ANY)],
        out_specs=pl.BlockSpec(memory_space=pl.ANY),
        out_shape=jax.ShapeDtypeStruct((n_rows, D), h.dtype),
        scratch_shapes=[pltpu.SemaphoreType.DMA(())],
        input_output_aliases={2: 0},
        compiler_params=_cparams("arbitrary"),
        name="moe_dispatch",
    )(dest_flat, h, xs0)


def _gmm_kernel(be_ref, nu_ref, xs_ref, wgu_ref, bgu_ref, wd_ref, bd_ref, y_ref):
    del be_ref
    i = pl.program_id(0)
    F = D_FF_EXPERT

    @pl.when(i < nu_ref[0])
    def _():
        x = xs_ref[...].astype(BF16)
        gu = _dot(x, wgu_ref[0]) + bgu_ref[0]
        g = jnp.minimum(gu[:, :F], SWIGLU_LIMIT)
        u = jnp.clip(gu[:, F:], -SWIGLU_LIMIT, SWIGLU_LIMIT)
        a = g * (1.0 / (1.0 + jnp.exp(-SWIGLU_ALPHA * g))) * (u + 1.0)
        y_ref[...] = _dot(a.astype(BF16), wd_ref[0]) + bd_ref[0]

    @pl.when(i >= nu_ref[0])
    def _():
        y_ref[...] = jnp.zeros_like(y_ref)


def _gmm(block_expert, n_used, xs, w_gu, b_gu, w_down, b_down, tm):
    n_rows, D = xs.shape
    E, _, F2 = w_gu.shape
    F = F2 // 2
    n_blocks = n_rows // tm

    def row_map(i, be, nu):
        return (jnp.minimum(i, nu[0] - 1), 0)

    def w_map(i, be, nu):
        return (be[i], 0, 0)

    return pl.pallas_call(
        _gmm_kernel,
        grid_spec=pltpu.PrefetchScalarGridSpec(
            num_scalar_prefetch=2,
            grid=(n_blocks,),
            in_specs=[pl.BlockSpec((tm, D), row_map),
                      pl.BlockSpec((1, D, F2), w_map),
                      pl.BlockSpec((1, 1, F2), w_map),
                      pl.BlockSpec((1, F, D), w_map),
                      pl.BlockSpec((1, 1, D), w_map)],
            out_specs=pl.BlockSpec((tm, D), lambda i, be, nu: (i, 0)),
        ),
        out_shape=jax.ShapeDtypeStruct((n_rows, D), F32),
        compiler_params=_cparams("arbitrary"),
        name="moe_expert_mlp",
    )(block_expert, n_used, xs, w_gu, b_gu.reshape(E, 1, F2), w_down, b_down.reshape(E, 1, D))


def _combine_kernel(dest_ref, gates_ref, x_ref, gm_ref, gf_ref, y_hbm, o_ref, ybuf, sem, *, tt, final):
    def issue(r, c):
        for k in range(TOP_K):
            _row_copy(y_hbm, dest_ref[r * TOP_K + k], ybuf.at[k], r, sem).start()
        return c

    lax.fori_loop(0, tt, issue, 0)

    def drain(r, c):
        for k in range(TOP_K):
            _row_copy(y_hbm, 0, ybuf.at[k], 0, sem).wait()
        return c

    lax.fori_loop(0, tt, drain, 0)

    g = gates_ref[...]
    moe = g[:, 0:1] * ybuf[0]
    for k in range(1, TOP_K):
        moe = moe + g[:, k:k + 1] * ybuf[k]
    xn = x_ref[...] + gm_ref[0] * moe
    if final:
        xn = _rms(xn, gf_ref[...])
    o_ref[...] = xn


def _combine(dest_flat, gates, x, gate_m, g_final, y, S, final):
    T, D = x.shape
    B = T // S
    tt = min(S, 128)
    per_b = S // tt
    row = pl.BlockSpec((tt, D), lambda i: (i, 0))
    return pl.pallas_call(
        functools.partial(_combine_kernel, tt=tt, final=final),
        grid=(T // tt,),
        in_specs=[pl.BlockSpec((tt * TOP_K,), lambda i: (i,), memory_space=pltpu.SMEM),
                  pl.BlockSpec((tt, TOP_K), lambda i: (i, 0)),
                  row,
                  pl.BlockSpec((1, 1, D), lambda i: (i // per_b, 0, 0)),
                  pl.BlockSpec((1, D), lambda i: (0, 0)),
                  pl.BlockSpec(memory_space=pl.ANY)],
        out_specs=row,
        out_shape=jax.ShapeDtypeStruct((T, D), F32),
        scratch_shapes=[pltpu.VMEM((TOP_K, tt, D), F32), pltpu.SemaphoreType.DMA(())],
        compiler_params=_cparams("arbitrary"),
        name="moe_combine",
    )(dest_flat, gates, x, gate_m.reshape(B, 1, D), g_final.reshape(1, D), y)


def _moe(h, logits_t, x, gate_m, g_final, w_gu, b_gu, w_down, b_down, S, final):
    T, D = h.shape
    E = N_EXPERTS
    tm = 512
    idx_t, gates_t, rank_t, cnt = _route(logits_t)
    counts = cnt[:, 0].astype(I32)
    padded = (counts + tm - 1) // tm * tm
    pends = jnp.cumsum(padded)
    pstart = pends - padded
    n_rows = T * TOP_K + E * tm
    n_blocks = n_rows // tm
    block_start = jnp.arange(n_blocks, dtype=I32) * tm
    block_expert = jnp.minimum(jnp.searchsorted(pends, block_start, side="right"), E - 1).astype(I32)
    n_used = (pends[-1:] // tm).astype(I32)
    dest_flat = _dest(idx_t, rank_t, pstart).T.reshape(-1)
    xs = _dispatch(dest_flat, h, n_rows)
    y = _gmm(block_expert, n_used, xs, w_gu, b_gu, w_down, b_down, tm)
    return _combine(dest_flat, gates_t.T, x, gate_m, g_final, y, S, final)


def _dq_kernel(x_ref, g_ref, sh_ref, sc_ref, w_ref, gcq_ref, o_ref):
    h = _norm_mod(x_ref[...], g_ref[...], sh_ref[0], sc_ref[0]).astype(BF16)
    o_ref[...] = _rms(_dot(h, w_ref[...]), gcq_ref[...]).astype(o_ref.dtype)


def _dq(x, g, shift, scale, w, gcq, S):
    T, D = x.shape
    R = w.shape[1]
    B = T // S
    tm = min(S, 512)
    per_b = S // tm
    mod = pl.BlockSpec((1, 1, D), lambda i: (i // per_b, 0, 0))
    return pl.pallas_call(
        _dq_kernel, grid=(T // tm,),
        in_specs=[pl.BlockSpec((tm, D), lambda i: (i, 0)),
                  pl.BlockSpec((1, D), lambda i: (0, 0)), mod, mod,
                  pl.BlockSpec((D, R), lambda i: (0, 0)),
                  pl.BlockSpec((1, R), lambda i: (0, 0))],
        out_specs=pl.BlockSpec((tm, R), lambda i: (i, 0)),
        out_shape=jax.ShapeDtypeStruct((T, R), BF16),
        compiler_params=_cparams("parallel"), name="q_down_norm",
    )(x, g.reshape(1, D), shift.reshape(B, 1, D), scale.reshape(B, 1, D), w, gcq.reshape(1, R))


def _uq_rope_kernel(cq_ref, w_ref, cos_ref, sin_ref, o_ref):
    acc = _dot(cq_ref[...], w_ref[0])
    o_ref[...] = (acc * cos_ref[...] + pltpu.roll(acc, QPAD - QK_ROPE_DIM, 1) * sin_ref[...]).astype(o_ref.dtype)


def _uq_rope(cq, wq, cos_p, sin_p):
    T, R = cq.shape
    H = wq.shape[0]
    tm = min(T, 512)
    tab = pl.BlockSpec((tm, QPAD), lambda i, h: (i, 0))
    return pl.pallas_call(
        _uq_rope_kernel, grid=(T // tm, H),
        in_specs=[pl.BlockSpec((tm, R), lambda i, h: (i, 0)),
                  pl.BlockSpec((1, R, QPAD), lambda i, h: (h, 0, 0)), tab, tab],
        out_specs=pl.BlockSpec((tm, QPAD), lambda i, h: (i, h)),
        out_shape=jax.ShapeDtypeStruct((T, H * QPAD), BF16),
        compiler_params=_cparams("parallel", "arbitrary"), name="q_up_rope",
    )(cq, wq, cos_p, sin_p)


def _kv_latent_kernel(x_ref, g_ref, sh_ref, sc_ref, w_ref, gc_ref, cos_ref, sin_ref, ckv_ref, kr_ref):
    h = _norm_mod(x_ref[...], g_ref[...], sh_ref[0], sc_ref[0]).astype(BF16)
    lat = _dot(h, w_ref[...])
    ckv_ref[...] = _rms(lat[:, :KV_LORA_RANK], gc_ref[...]).astype(ckv_ref.dtype)
    t = lat[:, KV_LORA_RANK:]
    kr_ref[...] = (t * cos_ref[...] + pltpu.roll(t, QK_ROPE_DIM, 1) * sin_ref[...]).astype(kr_ref.dtype)


def _kv_latent(x, g, shift, scale, w_ext, g_ckv, cos_k, sin_k, S):
    T, D = x.shape
    B = T // S
    N = w_ext.shape[1]
    tm = min(S, 512)
    per_b = S // tm
    mod = pl.BlockSpec((1, 1, D), lambda i: (i // per_b, 0, 0))
    tab = pl.BlockSpec((tm, LANES), lambda i: (i, 0))
    return pl.pallas_call(
        _kv_latent_kernel, grid=(T // tm,),
        in_specs=[pl.BlockSpec((tm, D), lambda i: (i, 0)),
                  pl.BlockSpec((1, D), lambda i: (0, 0)), mod, mod,
                  pl.BlockSpec((D, N), lambda i: (0, 0)),
                  pl.BlockSpec((1, KV_LORA_RANK), lambda i: (0, 0)), tab, tab],
        out_specs=[pl.BlockSpec((tm, KV_LORA_RANK), lambda i: (i, 0)), tab],
        out_shape=[jax.ShapeDtypeStruct((T, KV_LORA_RANK), BF16),
                   jax.ShapeDtypeStruct((T, LANES), BF16)],
        compiler_params=_cparams("parallel"), name="kv_latent",
    )(x, g.reshape(1, D), shift.reshape(B, 1, D), scale.reshape(B, 1, D), w_ext,
      g_ckv.reshape(1, KV_LORA_RANK), cos_k, sin_k)


def _kv_up_kernel(c_ref, wk_ref, wv_ref, kr_ref, k_ref, v_ref):
    c = c_ref[...]
    k_ref[:, :QK_NOPE_DIM] = _dot(c, wk_ref[...]).astype(k_ref.dtype)
    k_ref[:, QK_NOPE_DIM:] = kr_ref[...]
    v_ref[...] = _dot(c, wv_ref[...]).astype(v_ref.dtype)


def _kv_up(ckv, w_uk, w_uv, k_rope):
    T, R = ckv.shape
    H = MLA_HEADS
    tm = min(T, 512)
    return pl.pallas_call(
        _kv_up_kernel, grid=(T // tm, H),
        in_specs=[pl.BlockSpec((tm, R), lambda i, h: (i, 0)),
                  pl.BlockSpec((R, QK_NOPE_DIM), lambda i, h: (0, h)),
                  pl.BlockSpec((R, V_HEAD_DIM), lambda i, h: (0, h)),
                  pl.BlockSpec((tm, LANES), lambda i, h: (i, 0))],
        out_specs=[pl.BlockSpec((tm, QPAD), lambda i, h: (i, h)),
                   pl.BlockSpec((tm, V_HEAD_DIM), lambda i, h: (i, h))],
        out_shape=[jax.ShapeDtypeStruct((T, H * QPAD), BF16),
                   jax.ShapeDtypeStruct((T, H * V_HEAD_DIM), BF16)],
        compiler_params=_cparams("parallel", "arbitrary"), name="kv_up",
    )(ckv, w_uk, w_uv, k_rope)


def _mla_attn_kernel(q_ref, k_ref, v_ref, o_ref, m_ref, l_ref, acc_ref, *, tq):
    qi = pl.program_id(2)
    q = q_ref[...]
    m_ref[...] = jnp.full_like(m_ref, NEG_INF)
    l_ref[...] = jnp.zeros_like(l_ref)
    acc_ref[...] = jnp.zeros_like(acc_ref)

    def step(ks, masked):
        kb = k_ref[pl.ds(ks, tq), :]
        vb = v_ref[pl.ds(ks, tq), :]
        s = _dot_nt(q, kb)
        if masked:
            qc = lax.broadcasted_iota(I32, (tq, tq), 0) // CHUNK
            kc = lax.broadcasted_iota(I32, (tq, tq), 1) // CHUNK
            s = jnp.where(kc <= qc, s, NEG_INF)
        m_prev = m_ref[...]
        m_new = jnp.maximum(m_prev, jnp.max(s, axis=1, keepdims=True))
        alpha = jnp.exp(m_prev - m_new)
        p = jnp.exp(s - m_new)
        l_ref[...] = alpha * l_ref[...] + jnp.sum(p, axis=1, keepdims=True)
        acc_ref[...] = alpha * acc_ref[...] + _dot(p.astype(BF16), vb)
        m_ref[...] = m_new

    def body(j, c):
        step(pl.multiple_of(j * tq, tq), False)
        return c

    lax.fori_loop(0, qi, body, 0)
    step(pl.multiple_of(qi * tq, tq), True)
    o_ref[...] = (acc_ref[...] / l_ref[...]).astype(o_ref.dtype)


def _mla_attention(q_cat, k_cat, v, B, S):
    H = MLA_HEADS
    T = q_cat.shape[0]
    tq = min(S, 256)
    nq = S // tq
    return pl.pallas_call(
        functools.partial(_mla_attn_kernel, tq=tq),
        grid=(B, H, nq),
        in_specs=[pl.BlockSpec((tq, QPAD), lambda b, h, i: (b * nq + i, h)),
                  pl.BlockSpec((S, QPAD), lambda b, h, i: (b, h)),
                  pl.BlockSpec((S, V_HEAD_DIM), lambda b, h, i: (b, h))],
        out_specs=pl.BlockSpec((tq, V_HEAD_DIM), lambda b, h, i: (b * nq + i, h)),
        out_shape=jax.ShapeDtypeStruct((T, H * V_HEAD_DIM), BF16),
        scratch_shapes=[pltpu.VMEM((tq, 1), F32), pltpu.VMEM((tq, 1), F32),
                        pltpu.VMEM((tq, V_HEAD_DIM), F32)],
        compiler_params=_cparams("parallel", "parallel", "arbitrary"),
        name="mla_attention",
    )(q_cat, k_cat, v)


def _rot_cols(w):
    half = QK_ROPE_DIM // 2
    return jnp.concatenate([-w[..., half:], w[..., :half]], axis=-1)


def _q_weights(w_uq):
    R = w_uq.shape[0]
    w = w_uq.reshape(R, MLA_HEADS, QK_NOPE_DIM + QK_ROPE_DIM)
    wn, wr = w[..., :QK_NOPE_DIM], w[..., QK_NOPE_DIM:]
    return jnp.concatenate([wn, wr, _rot_cols(wr)], axis=-1).transpose(1, 0, 2).astype(BF16)


def kernel(x, c, positions, w_mod, b_mod, g_mix, g_ffn, w_qkv_a, w_o_a, g_kv_in, w_mod_kv, b_mod_kv,
           w_dkv, g_ckv, w_uk, w_uv, w_dq_b, g_cq_b, w_uq_b, w_o_b, router_w, router_b, w_gu, b_gu,
           w_down, b_down, g_final):
    B, S, D = x.shape
    T = B * S
    depth = w_mod.shape[0]
    n_a = w_qkv_a.shape[0]
    xt = x.reshape(T, D)

    mod = _adaln(c, w_mod, b_mod)
    mod_kv = _adaln(c, w_mod_kv[None], b_mod_kv[None])[0]

    half = QK_ROPE_DIM // 2
    inv_freq = ROPE_BASE ** (-jnp.arange(half, dtype=F32) / half)
    ang = positions.astype(F32).reshape(T, 1) * inv_freq
    cos_f, sin_f = _cossin(ang.reshape(T * half // LANES, LANES))
    cos = cos_f.reshape(T, half)
    sin = sin_f.reshape(T, half)
    z64 = jnp.zeros((T, QK_ROPE_DIM), F32)
    cos_k = jnp.concatenate([cos, cos, z64], axis=1)
    sin_k = jnp.concatenate([sin, sin, z64], axis=1)
    qs = (QK_NOPE_DIM + QK_ROPE_DIM) ** -0.5
    cos_q = jnp.concatenate([jnp.ones((T, QK_NOPE_DIM), F32), cos_k], axis=1) * qs
    sin_q = jnp.concatenate([jnp.zeros((T, QK_NOPE_DIM), F32), sin_k], axis=1) * qs

    w_gu_b = w_gu.astype(BF16)
    w_down_b = w_down.astype(BF16)

    shared = None
    for layer in range(depth):
        m = mod[layer]
        shift_a, scale_a, gate_a, shift_m, scale_m, gate_m = [m[:, i * D:(i + 1) * D] for i in range(6)]
        if layer < n_a:
            qkv = _nm_matmul(xt, g_mix[layer], shift_a, scale_a, w_qkv_a[layer].astype(BF16), S,
                             q_cols=SB_HEADS * SB_HEAD_DIM, q_scale=SB_HEAD_DIM ** -0.5)
            o = _sb_attention(qkv, B, S)
            w_o = w_o_a[layer]
        else:
            i = layer - n_a
            k_cat, v_b = shared
            cq = _dq(xt, g_mix[layer], shift_a, scale_a, w_dq_b[i].astype(BF16), g_cq_b[i], S)
            q_cat = _uq_rope(cq, _q_weights(w_uq_b[i]), cos_q, sin_q)
            o = _mla_attention(q_cat, k_cat, v_b, B, S)
            w_o = w_o_b[i]
        xt, h, logits_t = _oproj(o, w_o.astype(BF16), xt, gate_a, g_ffn[layer], shift_m, scale_m,
                                 router_w[layer], router_b[layer], S)
        xt = _moe(h, logits_t, xt, gate_m, g_final, w_gu_b[layer], b_gu[layer], w_down_b[layer],
                  b_down[layer], S, final=(layer == depth - 1))
        if layer == n_a - 1:
            w_ext = jnp.concatenate([w_dkv, _rot_cols(w_dkv[:, KV_LORA_RANK:])], axis=1).astype(BF16)
            ckv, k_rope = _kv_latent(xt, g_kv_in, mod_kv[:, :D], mod_kv[:, D:], w_ext, g_ckv,
                                     cos_k, sin_k, S)
            shared = _kv_up(ckv, w_uk.astype(BF16), w_uv.astype(BF16), k_rope)
    return xt.reshape(B, S, D)
```

```python
import functools

import jax
import jax.numpy as jnp
from jax import lax
from jax.experimental import pallas as pl
from jax.experimental.pallas import tpu as pltpu

F32 = jnp.float32
BF16 = jnp.bfloat16
I32 = jnp.int32

CHUNK = 64
SB_HEADS = 16
SB_HEAD_DIM = 128
MLA_HEADS = 16
QK_NOPE_DIM = 128
QK_ROPE_DIM = 64
V_HEAD_DIM = 128
KV_LORA_RANK = 512
ROPE_BASE = 10000.0
N_EXPERTS = 32
TOP_K = 4
D_FF_EXPERT = 768
SWIGLU_ALPHA = 1.702
SWIGLU_LIMIT = 7.0
RMS_EPS = 1e-6

V7X_VMEM_LIMIT_BYTES = 52 * 1024 * 1024
LANES = 128
QPAD = 256

NEG_INF = float("-inf")
SB_UNDERFLOW_LOG = -104.0


def _cparams(*sem):
    return pltpu.CompilerParams(dimension_semantics=sem, vmem_limit_bytes=V7X_VMEM_LIMIT_BYTES)


def _split_bf16(a):
    hi = a.astype(BF16)
    lo = (a - hi.astype(F32)).astype(BF16)
    return hi, lo


def _dot(a, b):
    return jnp.dot(a, b, preferred_element_type=F32)


def _dot_nt(a, b):
    return lax.dot_general(a, b, (((1,), (1,)), ((), ())), preferred_element_type=F32)


def _rms(x, g):
    ms = jnp.mean(x * x, axis=-1, keepdims=True)
    return x * lax.rsqrt(ms + RMS_EPS) * g


def _norm_mod(x, g, shift, scale):
    return _rms(x, g) * (1.0 + scale) + shift


def _adaln_kernel(c_ref, w_ref, b_ref, o_ref):
    c = c_ref[...]
    s = c * (1.0 / (1.0 + jnp.exp(-c)))
    s_hi, s_lo = _split_bf16(s)
    w_hi, w_lo = _split_bf16(w_ref[0])
    o_ref[0] = _dot(s_hi, w_hi) + _dot(s_lo, w_hi) + _dot(s_hi, w_lo) + b_ref[0]


def _adaln(c, w, b):
    L, D, N = w.shape
    B = c.shape[0]
    tn = 1024
    return pl.pallas_call(
        _adaln_kernel,
        grid=(L, N // tn),
        in_specs=[pl.BlockSpec((B, D), lambda l, j: (0, 0)),
                  pl.BlockSpec((1, D, tn), lambda l, j: (l, 0, j)),
                  pl.BlockSpec((1, 1, tn), lambda l, j: (l, 0, j))],
        out_specs=pl.BlockSpec((1, B, tn), lambda l, j: (l, 0, j)),
        out_shape=jax.ShapeDtypeStruct((L, B, N), F32),
        compiler_params=_cparams("parallel", "parallel"),
        name="adaln",
    )(c, w, b.reshape(L, 1, N))


def _cossin_kernel(a_ref, c_ref, s_ref):
    a = a_ref[...]
    c_ref[...] = jnp.cos(a)
    s_ref[...] = jnp.sin(a)


def _cossin(ang):
    R = ang.shape[0]
    tr = min(R, 512)
    spec = pl.BlockSpec((tr, LANES), lambda i: (i, 0))
    return pl.pallas_call(
        _cossin_kernel, grid=(R // tr,), in_specs=[spec], out_specs=[spec, spec],
        out_shape=[jax.ShapeDtypeStruct(ang.shape, F32)] * 2,
        compiler_params=_cparams("parallel"), name="cossin",
    )(ang)


def _nm_matmul_kernel(x_ref, g_ref, sh_ref, sc_ref, w_ref, o_ref, h_ref, *, n_scaled, col_scale):
    j = pl.program_id(1)

    @pl.when(j == 0)
    def _():
        h_ref[...] = _norm_mod(x_ref[...], g_ref[...], sh_ref[0], sc_ref[0]).astype(BF16)

    acc = _dot(h_ref[...], w_ref[...])
    s = jnp.where(j < n_scaled, col_scale, 1.0).astype(F32)
    o_ref[...] = (acc * s).astype(o_ref.dtype)


def _nm_matmul(x, g, shift, scale, w, S, *, q_cols=0, q_scale=1.0):
    T, D = x.shape
    N = w.shape[1]
    tm = min(S, 512)
    tn = 1024
    per_b = S // tm
    B = T // S
    mod_spec = pl.BlockSpec((1, 1, D), lambda i, j: (i // per_b, 0, 0))
    return pl.pallas_call(
        functools.partial(_nm_matmul_kernel, n_scaled=q_cols // tn, col_scale=q_scale),
        grid=(T // tm, N // tn),
        in_specs=[pl.BlockSpec((tm, D), lambda i, j: (i, 0)),
                  pl.BlockSpec((1, D), lambda i, j: (0, 0)),
                  mod_spec, mod_spec,
                  pl.BlockSpec((D, tn), lambda i, j: (0, j))],
        out_specs=pl.BlockSpec((tm, tn), lambda i, j: (i, j)),
        out_shape=jax.ShapeDtypeStruct((T, N), BF16),
        scratch_shapes=[pltpu.VMEM((tm, D), BF16)],
        compiler_params=_cparams("parallel", "arbitrary"),
        name="norm_mod_matmul",
    )(x, g.reshape(1, D), shift.reshape(B, 1, D), scale.reshape(B, 1, D), w)


def _sb_attn_kernel(q_ref, k_ref, v_ref, tri_ref, o_ref, acc_ref, run_ref, *, tq, tk, heads):
    qi = pl.program_id(2)
    Dh = SB_HEAD_DIM
    tri = tri_ref[...]
    n_kb = ((qi + 1) * tq) // tk
    n_diag = tq // tk
    t_loc = lax.broadcasted_iota(I32, (tq, tk), 0)
    s_loc = lax.broadcasted_iota(I32, (tq, tk), 1)
    acc_ref[...] = jnp.zeros_like(acc_ref)
    run_ref[...] = jnp.zeros_like(run_ref)

    def block(j, mask):
        ks = pl.multiple_of(j * tk, tk)
        heads_cols = [slice(g * Dh, (g + 1) * Dh) for g in range(heads)]
        zs = [_dot_nt(q_ref[:, c], k_ref[pl.ds(ks, tk), c]) for c in heads_cols]
        ps = []
        for z in zs:
            lk = jnp.minimum(-z, 0.0) - jnp.log(1.0 + jnp.exp(-jnp.abs(z)))
            if mask is not None:
                lk = jnp.where(mask, lk, 0.0)
            lk_hi, lk_lo = _split_bf16(lk)
            ps.append(_dot(lk_hi, tri) + _dot(lk_lo, tri))
        worst = None
        for g in range(heads):
            run = run_ref[g]
            a = jnp.exp(zs[g] + ps[g][:, :tk] + run)
            if mask is not None:
                a = jnp.where(mask, a, 0.0)
            acc_ref[g] += _dot(a.astype(BF16), v_ref[pl.ds(ks, tk), heads_cols[g]])
            run = run + ps[g][:, tk:]
            run_ref[g] = run
            m = jnp.max(run)
            worst = m if worst is None else jnp.maximum(worst, m)
        return worst

    for d in range(n_diag):
        j = n_kb - 1 - d
        worst = block(j, (j * tk - qi * tq + s_loc) < t_loc)

    def cond(c):
        j, worst = c
        return jnp.logical_and(j >= 0, worst > SB_UNDERFLOW_LOG)

    def body(c):
        j, _ = c
        return j - 1, block(j, None)

    lax.while_loop(cond, body, (n_kb - n_diag - 1, worst))
    for g in range(heads):
        o_ref[:, g * Dh:(g + 1) * Dh] = acc_ref[g].astype(o_ref.dtype)


def _sb_attention(qkv, B, S):
    H, Dh = SB_HEADS, SB_HEAD_DIM
    T = qkv.shape[0]
    tk = 128
    tq = min(S, 256)
    G = 4
    nq = S // tq
    hb = H // G
    jj = jnp.arange(tk)[:, None]
    ss = jnp.arange(tk + LANES)[None, :]
    tri = ((jj >= ss) | (ss >= tk)).astype(BF16)
    return pl.pallas_call(
        functools.partial(_sb_attn_kernel, tq=tq, tk=tk, heads=G),
        grid=(B, hb, nq),
        in_specs=[pl.BlockSpec((tq, G * Dh), lambda b, h, i: (b * nq + i, h)),
                  pl.BlockSpec((S, G * Dh), lambda b, h, i: (b, hb + h)),
                  pl.BlockSpec((S, G * Dh), lambda b, h, i: (b, 2 * hb + h)),
                  pl.BlockSpec((tk, tk + LANES), lambda b, h, i: (0, 0))],
        out_specs=pl.BlockSpec((tq, G * Dh), lambda b, h, i: (b * nq + i, h)),
        out_shape=jax.ShapeDtypeStruct((T, H * Dh), BF16),
        scratch_shapes=[pltpu.VMEM((G, tq, Dh), F32), pltpu.VMEM((G, tq, LANES), F32)],
        compiler_params=_cparams("parallel", "parallel", "arbitrary"),
        name="sb_attention",
    )(qkv, qkv, qkv, tri)


def _oproj_kernel(o_ref, w_ref, x_ref, ga_ref, g_ref, sh_ref, sc_ref, rwt_ref, rb_ref,
                  xo_ref, h_ref, lg_ref):
    mix = _dot(o_ref[...], w_ref[...])
    xn = x_ref[...] + ga_ref[0] * mix
    xo_ref[...] = xn
    h = _norm_mod(xn, g_ref[...], sh_ref[0], sc_ref[0])
    h_ref[...] = h
    h_hi, h_lo = _split_bf16(h)
    r_hi, r_lo = _split_bf16(rwt_ref[...])
    lg_ref[...] = _dot_nt(r_hi, h_hi) + _dot_nt(r_lo, h_hi) + _dot_nt(r_hi, h_lo) + rb_ref[...]


def _oproj(o, w, x, gate, g, shift, scale, router_w, router_b, S):
    T, D = x.shape
    E = router_w.shape[1]
    B = T // S
    tm = min(S, 256)
    per_b = S // tm
    row = pl.BlockSpec((tm, D), lambda i: (i, 0))
    mod = pl.BlockSpec((1, 1, D), lambda i: (i // per_b, 0, 0))
    return pl.pallas_call(
        _oproj_kernel,
        grid=(T // tm,),
        in_specs=[row,
                  pl.BlockSpec((D, D), lambda i: (0, 0)),
                  row, mod,
                  pl.BlockSpec((1, D), lambda i: (0, 0)),
                  mod, mod,
                  pl.BlockSpec((E, D), lambda i: (0, 0)),
                  pl.BlockSpec((E, 1), lambda i: (0, 0))],
        out_specs=[row, row, pl.BlockSpec((E, tm), lambda i: (0, i))],
        out_shape=[jax.ShapeDtypeStruct((T, D), F32),
                   jax.ShapeDtypeStruct((T, D), F32),
                   jax.ShapeDtypeStruct((E, T), F32)],
        compiler_params=_cparams("parallel"),
        name="oproj_residual_router",
    )(o, w, x, gate.reshape(B, 1, D), g.reshape(1, D), shift.reshape(B, 1, D),
      scale.reshape(B, 1, D), router_w.T, router_b.reshape(E, 1))


def _route_kernel(lg_ref, tri_ref, idx_ref, gate_ref, rank_ref, cnt_ref, carry_ref):
    i = pl.program_id(0)

    @pl.when(i == 0)
    def _():
        carry_ref[...] = jnp.zeros_like(carry_ref)

    work = lg_ref[...]
    E, tt = work.shape
    eio = lax.broadcasted_iota(I32, (E, tt), 0)
    vals, onehots = [], []
    for k in range(TOP_K):
        m = jnp.max(work, axis=0, keepdims=True)
        ix = jnp.min(jnp.where(work == m, eio, E), axis=0, keepdims=True)
        oh = eio == ix
        work = jnp.where(oh, NEG_INF, work)
        vals.append(m)
        onehots.append(oh)
        idx_ref[k:k + 1, :] = ix
    es = [jnp.exp(v - vals[0]) for v in vals]
    den = es[0] + es[1] + es[2] + es[3]
    for k in range(TOP_K):
        gate_ref[k:k + 1, :] = es[k] / den
    sel = jnp.zeros((E, tt), F32)
    for oh in onehots:
        sel = sel + jnp.where(oh, 1.0, 0.0)
    cnt = _dot(sel.astype(BF16), tri_ref[...]) + carry_ref[:, 0:1]
    for k in range(TOP_K):
        r = jnp.sum(jnp.where(onehots[k], cnt - 1.0, 0.0), axis=0, keepdims=True)
        rank_ref[k:k + 1, :] = r.astype(I32)
    last = jnp.broadcast_to(cnt[:, tt - 1:tt], (E, LANES))
    carry_ref[...] = last
    cnt_ref[...] = last


def _route(logits_t):
    E, T = logits_t.shape
    tt = min(T, 512)
    tri = (jnp.arange(tt)[:, None] <= jnp.arange(tt)[None, :]).astype(BF16)
    tok = pl.BlockSpec((TOP_K, tt), lambda i: (0, i))
    return pl.pallas_call(
        _route_kernel,
        grid=(T // tt,),
        in_specs=[pl.BlockSpec((E, tt), lambda i: (0, i)),
                  pl.BlockSpec((tt, tt), lambda i: (0, 0))],
        out_specs=[tok, tok, tok, pl.BlockSpec((E, LANES), lambda i: (0, 0))],
        out_shape=[jax.ShapeDtypeStruct((TOP_K, T), I32),
                   jax.ShapeDtypeStruct((TOP_K, T), F32),
                   jax.ShapeDtypeStruct((TOP_K, T), I32),
                   jax.ShapeDtypeStruct((E, LANES), F32)],
        scratch_shapes=[pltpu.VMEM((E, LANES), F32)],
        compiler_params=_cparams("arbitrary"),
        name="route_topk_rank",
    )(logits_t, tri)


def _dest_kernel(idx_ref, rank_ref, ps_ref, dest_ref):
    E = ps_ref.shape[0]
    tt = idx_ref.shape[1]
    eio = lax.broadcasted_iota(I32, (E, tt), 0)
    ps = ps_ref[...]
    for k in range(TOP_K):
        oh = eio == idx_ref[k:k + 1, :]
        base = jnp.sum(jnp.where(oh, ps, 0), axis=0, keepdims=True)
        dest_ref[k:k + 1, :] = base + rank_ref[k:k + 1, :]


def _dest(idx_t, rank_t, pstart):
    T = idx_t.shape[1]
    E = pstart.shape[0]
    tt = min(T, 2048)
    tok = pl.BlockSpec((TOP_K, tt), lambda i: (0, i))
    return pl.pallas_call(
        _dest_kernel, grid=(T // tt,),
        in_specs=[tok, tok, pl.BlockSpec((E, 1), lambda i: (0, 0))],
        out_specs=tok,
        out_shape=jax.ShapeDtypeStruct((TOP_K, T), I32),
        compiler_params=_cparams("parallel"), name="route_dest",
    )(idx_t, rank_t, pstart.reshape(E, 1))


def _row_copy(src, s_row, dst, d_row, sem):
    return pltpu.make_async_copy(src.at[pl.ds(s_row, 1)], dst.at[pl.ds(d_row, 1)], sem)


def _dispatch_kernel(dest_ref, h_ref, xs_in, xs_hbm, sem, *, tt):
    del xs_in

    def issue(r, c):
        for k in range(TOP_K):
            _row_copy(h_ref, r, xs_hbm, dest_ref[r * TOP_K + k], sem).start()
        return c

    lax.fori_loop(0, tt, issue, 0)

    def drain(r, c):
        for k in range(TOP_K):
            _row_copy(h_ref, 0, xs_hbm, 0, sem).wait()
        return c

    lax.fori_loop(0, tt, drain, 0)


def _dispatch(dest_flat, h, n_rows):
    T, D = h.shape
    tt = min(T, 256)
    xs0 = jnp.zeros((n_rows, D), h.dtype)
    return pl.pallas_call(
        functools.partial(_dispatch_kernel, tt=tt),
        grid=(T // tt,),
        in_specs=[pl.BlockSpec((tt * TOP_K,), lambda i: (i,), memory_space=pltpu.SMEM),
                  pl.BlockSpec((tt, D), lambda i: (i, 0)),
                  pl.BlockSpec(memory_space=pl.ANY)],
        out_specs=pl.BlockSpec(memory_space=pl.ANY),
        out_shape=jax.ShapeDtypeStruct((n_rows, D), h.dtype),
        scratch_shapes=[pltpu.SemaphoreType.DMA(())],
        input_output_aliases={2: 0},
        compiler_params=_cparams("arbitrary"),
        name="moe_dispatch",
    )(dest_flat, h, xs0)


def _gmm_kernel(be_ref, nu_ref, xs_ref, wgu_ref, bgu_ref, wd_ref, bd_ref, y_ref):
    del be_ref
    i = pl.program_id(0)
    F = D_FF_EXPERT

    @pl.when(i < nu_ref[0])
    def _():
        x = xs_ref[...].astype(BF16)
        gu = _dot(x, wgu_ref[0]) + bgu_ref[0]
        g = jnp.minimum(gu[:, :F], SWIGLU_LIMIT)
        u = jnp.clip(gu[:, F:], -SWIGLU_LIMIT, SWIGLU_LIMIT)
        a = g * (1.0 / (1.0 + jnp.exp(-SWIGLU_ALPHA * g))) * (u + 1.0)
        y_ref[...] = _dot(a.astype(BF16), wd_ref[0]) + bd_ref[0]

    @pl.when(i >= nu_ref[0])
    def _():
        y_ref[...] = jnp.zeros_like(y_ref)


def _gmm(block_expert, n_used, xs, w_gu, b_gu, w_down, b_down, tm):
    n_rows, D = xs.shape
    E, _, F2 = w_gu.shape
    F = F2 // 2
    n_blocks = n_rows // tm

    def row_map(i, be, nu):
        return (jnp.minimum(i, nu[0] - 1), 0)

    def w_map(i, be, nu):
        return (be[i], 0, 0)

    return pl.pallas_call(
        _gmm_kernel,
        grid_spec=pltpu.PrefetchScalarGridSpec(
            num_scalar_prefetch=2,
            grid=(n_blocks,),
            in_specs=[pl.BlockSpec((tm, D), row_map),
                      pl.BlockSpec((1, D, F2), w_map),
                      pl.BlockSpec((1, 1, F2), w_map),
                      pl.BlockSpec((1, F, D), w_map),
                      pl.BlockSpec((1, 1, D), w_map)],
            out_specs=pl.BlockSpec((tm, D), lambda i, be, nu: (i, 0)),
        ),
        out_shape=jax.ShapeDtypeStruct((n_rows, D), F32),
        compiler_params=_cparams("arbitrary"),
        name="moe_expert_mlp",
    )(block_expert, n_used, xs, w_gu, b_gu.reshape(E, 1, F2), w_down, b_down.reshape(E, 1, D))


def _combine_kernel(dest_ref, gates_ref, x_ref, gm_ref, gf_ref, y_hbm, o_ref, ybuf, sem, *, tt, final):
    def issue(r, c):
        for k in range(TOP_K):
            _row_copy(y_hbm, dest_ref[r * TOP_K + k], ybuf.at[k], r, sem).start()
        return c

    lax.fori_loop(0, tt, issue, 0)

    def drain(r, c):
        for k in range(TOP_K):
            _row_copy(y_hbm, 0, ybuf.at[k], 0, sem).wait()
        return c

    lax.fori_loop(0, tt, drain, 0)

    g = gates_ref[...]
    moe = g[:, 0:1] * ybuf[0]
    for k in range(1, TOP_K):
        moe = moe + g[:, k:k + 1] * ybuf[k]
    xn = x_ref[...] + gm_ref[0] * moe
    if final:
        xn = _rms(xn, gf_ref[...])
    o_ref[...] = xn


def _combine(dest_flat, gates, x, gate_m, g_final, y, S, final):
    T, D = x.shape
    B = T // S
    tt = min(S, 128)
    per_b = S // tt
    row = pl.BlockSpec((tt, D), lambda i: (i, 0))
    return pl.pallas_call(
        functools.partial(_combine_kernel, tt=tt, final=final),
        grid=(T // tt,),
        in_specs=[pl.BlockSpec((tt * TOP_K,), lambda i: (i,), memory_space=pltpu.SMEM),
                  pl.BlockSpec((tt, TOP_K), lambda i: (i, 0)),
                  row,
                  pl.BlockSpec((1, 1, D), lambda i: (i // per_b, 0, 0)),
                  pl.BlockSpec((1, D), lambda i: (0, 0)),
                  pl.BlockSpec(memory_space=pl.ANY)],
        out_specs=row,
        out_shape=jax.ShapeDtypeStruct((T, D), F32),
        scratch_shapes=[pltpu.VMEM((TOP_K, tt, D), F32), pltpu.SemaphoreType.DMA(())],
        compiler_params=_cparams("arbitrary"),
        name="moe_combine",
    )(dest_flat, gates, x, gate_m.reshape(B, 1, D), g_final.reshape(1, D), y)


def _moe(h, logits_t, x, gate_m, g_final, w_gu, b_gu, w_down, b_down, S, final):
    T, D = h.shape
    E = N_EXPERTS
    tm = 512
    idx_t, gates_t, rank_t, cnt = _route(logits_t)
    counts = cnt[:, 0].astype(I32)
    padded = (counts + tm - 1) // tm * tm
    pends = jnp.cumsum(padded)
    pstart = pends - padded
    n_rows = T * TOP_K + E * tm
    n_blocks = n_rows // tm
    block_start = jnp.arange(n_blocks, dtype=I32) * tm
    block_expert = jnp.minimum(jnp.searchsorted(pends, block_start, side="right"), E - 1).astype(I32)
    n_used = (pends[-1:] // tm).astype(I32)
    dest_flat = _dest(idx_t, rank_t, pstart).T.reshape(-1)
    xs = _dispatch(dest_flat, h, n_rows)
    y = _gmm(block_expert, n_used, xs, w_gu, b_gu, w_down, b_down, tm)
    return _combine(dest_flat, gates_t.T, x, gate_m, g_final, y, S, final)


def _dq_kernel(x_ref, g_ref, sh_ref, sc_ref, w_ref, gcq_ref, o_ref):
    h = _norm_mod(x_ref[...], g_ref[...], sh_ref[0], sc_ref[0]).astype(BF16)
    o_ref[...] = _rms(_dot(h, w_ref[...]), gcq_ref[...]).astype(o_ref.dtype)


def _dq(x, g, shift, scale, w, gcq, S):
    T, D = x.shape
    R = w.shape[1]
    B = T // S
    tm = min(S, 512)
    per_b = S // tm
    mod = pl.BlockSpec((1, 1, D), lambda i: (i // per_b, 0, 0))
    return pl.pallas_call(
        _dq_kernel, grid=(T // tm,),
        in_specs=[pl.BlockSpec((tm, D), lambda i: (i, 0)),
                  pl.BlockSpec((1, D), lambda i: (0, 0)), mod, mod,
                  pl.BlockSpec((D, R), lambda i: (0, 0)),
                  pl.BlockSpec((1, R), lambda i: (0, 0))],
        out_specs=pl.BlockSpec((tm, R), lambda i: (i, 0)),
        out_shape=jax.ShapeDtypeStruct((T, R), BF16),
        compiler_params=_cparams("parallel"), name="q_down_norm",
    )(x, g.reshape(1, D), shift.reshape(B, 1, D), scale.reshape(B, 1, D), w, gcq.reshape(1, R))


def _uq_rope_kernel(cq_ref, w_ref, cos_ref, sin_ref, o_ref):
    acc = _dot(cq_ref[...], w_ref[0])
    o_ref[...] = (acc * cos_ref[...] + pltpu.roll(acc, QPAD - QK_ROPE_DIM, 1) * sin_ref[...]).astype(o_ref.dtype)


def _uq_rope(cq, wq, cos_p, sin_p):
    T, R = cq.shape
    H = wq.shape[0]
    tm = min(T, 512)
    tab = pl.BlockSpec((tm, QPAD), lambda i, h: (i, 0))
    return pl.pallas_call(
        _uq_rope_kernel, grid=(T // tm, H),
        in_specs=[pl.BlockSpec((tm, R), lambda i, h: (i, 0)),
                  pl.BlockSpec((1, R, QPAD), lambda i, h: (h, 0, 0)), tab, tab],
        out_specs=pl.BlockSpec((tm, QPAD), lambda i, h: (i, h)),
        out_shape=jax.ShapeDtypeStruct((T, H * QPAD), BF16),
        compiler_params=_cparams("parallel", "arbitrary"), name="q_up_rope",
    )(cq, wq, cos_p, sin_p)


def _kv_latent_kernel(x_ref, g_ref, sh_ref, sc_ref, w_ref, gc_ref, cos_ref, sin_ref, ckv_ref, kr_ref):
    h = _norm_mod(x_ref[...], g_ref[...], sh_ref[0], sc_ref[0]).astype(BF16)
    lat = _dot(h, w_ref[...])
    ckv_ref[...] = _rms(lat[:, :KV_LORA_RANK], gc_ref[...]).astype(ckv_ref.dtype)
    t = lat[:, KV_LORA_RANK:]
    kr_ref[...] = (t * cos_ref[...] + pltpu.roll(t, QK_ROPE_DIM, 1) * sin_ref[...]).astype(kr_ref.dtype)


def _kv_latent(x, g, shift, scale, w_ext, g_ckv, cos_k, sin_k, S):
    T, D = x.shape
    B = T // S
    N = w_ext.shape[1]
    tm = min(S, 512)
    per_b = S // tm
    mod = pl.BlockSpec((1, 1, D), lambda i: (i // per_b, 0, 0))
    tab = pl.BlockSpec((tm, LANES), lambda i: (i, 0))
    return pl.pallas_call(
        _kv_latent_kernel, grid=(T // tm,),
        in_specs=[pl.BlockSpec((tm, D), lambda i: (i, 0)),
                  pl.BlockSpec((1, D), lambda i: (0, 0)), mod, mod,
                  pl.BlockSpec((D, N), lambda i: (0, 0)),
                  pl.BlockSpec((1, KV_LORA_RANK), lambda i: (0, 0)), tab, tab],
        out_specs=[pl.BlockSpec((tm, KV_LORA_RANK), lambda i: (i, 0)), tab],
        out_shape=[jax.ShapeDtypeStruct((T, KV_LORA_RANK), BF16),
                   jax.ShapeDtypeStruct((T, LANES), BF16)],
        compiler_params=_cparams("parallel"), name="kv_latent",
    )(x, g.reshape(1, D), shift.reshape(B, 1, D), scale.reshape(B, 1, D), w_ext,
      g_ckv.reshape(1, KV_LORA_RANK), cos_k, sin_k)


def _kv_up_kernel(c_ref, wk_ref, wvt_ref, kr_ref, k_ref, vt_ref, *, tk):
    c = c_ref[...]
    kn = _dot(c, wk_ref[...]).astype(k_ref.dtype)
    kr = kr_ref[...]
    for h in range(MLA_HEADS):
        k_ref[:, h * QPAD:h * QPAD + QK_NOPE_DIM] = kn[:, h * QK_NOPE_DIM:(h + 1) * QK_NOPE_DIM]
        k_ref[:, h * QPAD + QK_NOPE_DIM:(h + 1) * QPAD] = kr
    vt = _dot_nt(wvt_ref[...], c).astype(vt_ref.dtype)
    for j in range(vt_ref.shape[0]):
        vt_ref[j] = vt[:, j * tk:(j + 1) * tk]


def _kv_up(ckv, w_uk, w_uv_t, k_rope, tk):
    T, R = ckv.shape
    H = MLA_HEADS
    tm = min(T, 512)
    return pl.pallas_call(
        functools.partial(_kv_up_kernel, tk=tk), grid=(T // tm,),
        in_specs=[pl.BlockSpec((tm, R), lambda i: (i, 0)),
                  pl.BlockSpec((R, H * QK_NOPE_DIM), lambda i: (0, 0)),
                  pl.BlockSpec((H * V_HEAD_DIM, R), lambda i: (0, 0)),
                  pl.BlockSpec((tm, LANES), lambda i: (i, 0))],
        out_specs=[pl.BlockSpec((tm, H * QPAD), lambda i: (i, 0)),
                   pl.BlockSpec((tm // tk, H * V_HEAD_DIM, tk), lambda i: (i, 0, 0))],
        out_shape=[jax.ShapeDtypeStruct((T, H * QPAD), BF16),
                   jax.ShapeDtypeStruct((T // tk, H * V_HEAD_DIM, tk), BF16)],
        compiler_params=_cparams("parallel"), name="kv_up",
    )(ckv, w_uk, w_uv_t, k_rope)


def _mla_attn_kernel(q_ref, k_ref, vt_ref, o_ref, m_ref, l_ref, acc_ref, *, tq, heads):
    qi = pl.program_id(2)
    Dv = V_HEAD_DIM
    m_ref[...] = jnp.full_like(m_ref, NEG_INF)
    l_ref[...] = jnp.zeros_like(l_ref)
    acc_ref[...] = jnp.zeros_like(acc_ref)

    def step(j, masked):
        ks = pl.multiple_of(j * tq, tq)
        state = [(m_ref[g], l_ref[g], acc_ref[g]) for g in range(heads)]
        new_state = []
        scores = []
        for g in range(heads):
            qcols = slice(g * QPAD, (g + 1) * QPAD)
            scores.append(_dot_nt(k_ref[pl.ds(ks, tq), qcols], q_ref[:, qcols]))
        for g in range(heads):
            st = scores[g]
            if masked:
                kc = lax.broadcasted_iota(I32, (tq, tq), 0) // CHUNK
                qc = lax.broadcasted_iota(I32, (tq, tq), 1) // CHUNK
                st = jnp.where(kc <= qc, st, NEG_INF)
            m_prev, l_prev, acc_prev = state[g]
            m_new = jnp.maximum(m_prev, jnp.max(st, axis=0, keepdims=True))
            alpha = jnp.exp(m_prev - m_new)
            p = jnp.exp(st - m_new)
            l_new = alpha * l_prev + jnp.sum(p, axis=0, keepdims=True)
            acc_new = alpha * acc_prev + _dot(vt_ref[j, g * Dv:(g + 1) * Dv, :], p.astype(BF16))
            new_state.append((m_new, l_new, acc_new))
        for g in range(heads):
            m_ref[g], l_ref[g], acc_ref[g] = new_state[g]

    def body(j, c):
        step(j, False)
        return c

    lax.fori_loop(0, qi, body, 0)
    step(qi, True)
    for g in range(heads):
        o_ref[:, g * Dv:(g + 1) * Dv] = (acc_ref[g] / l_ref[g]).T.astype(o_ref.dtype)


def _mla_attention(q_cat, k_cat, v_t, B, S, tq):
    H = MLA_HEADS
    T = q_cat.shape[0]
    G = 4
    nq = S // tq
    return pl.pallas_call(
        functools.partial(_mla_attn_kernel, tq=tq, heads=G),
        grid=(B, H // G, nq),
        in_specs=[pl.BlockSpec((tq, G * QPAD), lambda b, h, i: (b * nq + i, h)),
                  pl.BlockSpec((S, G * QPAD), lambda b, h, i: (b, h)),
                  pl.BlockSpec((nq, G * V_HEAD_DIM, tq), lambda b, h, i: (b, h, 0))],
        out_specs=pl.BlockSpec((tq, G * V_HEAD_DIM), lambda b, h, i: (b * nq + i, h)),
        out_shape=jax.ShapeDtypeStruct((T, H * V_HEAD_DIM), BF16),
        scratch_shapes=[pltpu.VMEM((G, 1, tq), F32), pltpu.VMEM((G, 1, tq), F32),
                        pltpu.VMEM((G, V_HEAD_DIM, tq), F32)],
        compiler_params=_cparams("parallel", "parallel", "arbitrary"),
        name="mla_attention",
    )(q_cat, k_cat, v_t)


def _rot_cols(w):
    half = QK_ROPE_DIM // 2
    return jnp.concatenate([-w[..., half:], w[..., :half]], axis=-1)


def _q_weights(w_uq):
    R = w_uq.shape[0]
    w = w_uq.reshape(R, MLA_HEADS, QK_NOPE_DIM + QK_ROPE_DIM)
    wn, wr = w[..., :QK_NOPE_DIM], w[..., QK_NOPE_DIM:]
    return jnp.concatenate([wn, wr, _rot_cols(wr)], axis=-1).transpose(1, 0, 2).astype(BF16)


def kernel(x, c, positions, w_mod, b_mod, g_mix, g_ffn, w_qkv_a, w_o_a, g_kv_in, w_mod_kv, b_mod_kv,
           w_dkv, g_ckv, w_uk, w_uv, w_dq_b, g_cq_b, w_uq_b, w_o_b, router_w, router_b, w_gu, b_gu,
           w_down, b_down, g_final):
    B, S, D = x.shape
    T = B * S
    depth = w_mod.shape[0]
    n_a = w_qkv_a.shape[0]
    xt = x.reshape(T, D)

    mod = _adaln(c, w_mod, b_mod)
    mod_kv = _adaln(c, w_mod_kv[None], b_mod_kv[None])[0]

    half = QK_ROPE_DIM // 2
    inv_freq = ROPE_BASE ** (-jnp.arange(half, dtype=F32) / half)
    ang = positions.astype(F32).reshape(T, 1) * inv_freq
    cos_f, sin_f = _cossin(ang.reshape(T * half // LANES, LANES))
    cos = cos_f.reshape(T, half)
    sin = sin_f.reshape(T, half)
    z64 = jnp.zeros((T, QK_ROPE_DIM), F32)
    cos_k = jnp.concatenate([cos, cos, z64], axis=1)
    sin_k = jnp.concatenate([sin, sin, z64], axis=1)
    qs = (QK_NOPE_DIM + QK_ROPE_DIM) ** -0.5
    cos_q = jnp.concatenate([jnp.ones((T, QK_NOPE_DIM), F32), cos_k], axis=1) * qs
    sin_q = jnp.concatenate([jnp.zeros((T, QK_NOPE_DIM), F32), sin_k], axis=1) * qs

    tq_mla = min(S, 256)
    w_gu_b = w_gu.astype(BF16)
    w_down_b = w_down.astype(BF16)

    shared = None
    for layer in range(depth):
        m = mod[layer]
        shift_a, scale_a, gate_a, shift_m, scale_m, gate_m = [m[:, i * D:(i + 1) * D] for i in range(6)]
        if layer < n_a:
            qkv = _nm_matmul(xt, g_mix[layer], shift_a, scale_a, w_qkv_a[layer].astype(BF16), S,
                             q_cols=SB_HEADS * SB_HEAD_DIM, q_scale=SB_HEAD_DIM ** -0.5)
            o = _sb_attention(qkv, B, S)
            w_o = w_o_a[layer]
        else:
            i = layer - n_a
            k_cat, v_b = shared
            cq = _dq(xt, g_mix[layer], shift_a, scale_a, w_dq_b[i].astype(BF16), g_cq_b[i], S)
            q_cat = _uq_rope(cq, _q_weights(w_uq_b[i]), cos_q, sin_q)
            o = _mla_attention(q_cat, k_cat, v_b, B, S, tq_mla)
            w_o = w_o_b[i]
        xt, h, logits_t = _oproj(o, w_o.astype(BF16), xt, gate_a, g_ffn[layer], shift_m, scale_m,
                                 router_w[layer], router_b[layer], S)
        xt = _moe(h, logits_t, xt, gate_m, g_final, w_gu_b[layer], b_gu[layer], w_down_b[layer],
                  b_down[layer], S, final=(layer == depth - 1))
        if layer == n_a - 1:
            w_ext = jnp.concatenate([w_dkv, _rot_cols(w_dkv[:, KV_LORA_RANK:])], axis=1).astype(BF16)
            ckv, k_rope = _kv_latent(xt, g_kv_in, mod_kv[:, :D], mod_kv[:, D:], w_ext, g_ckv,
                                     cos_k, sin_k, S)
            shared = _kv_up(ckv, w_uk.astype(BF16), w_uv.T.astype(BF16), k_rope, tq_mla)
    return xt.reshape(B, S, D)
```

```python
import functools

import jax
import jax.numpy as jnp
from jax import lax
from jax.experimental import pallas as pl
from jax.experimental.pallas import tpu as pltpu

F32 = jnp.float32
BF16 = jnp.bfloat16
I32 = jnp.int32

CHUNK = 64
SB_HEADS = 16
SB_HEAD_DIM = 128
MLA_HEADS = 16
QK_NOPE_DIM = 128
QK_ROPE_DIM = 64
V_HEAD_DIM = 128
KV_LORA_RANK = 512
ROPE_BASE = 10000.0
N_EXPERTS = 32
TOP_K = 4
D_FF_EXPERT = 768
SWIGLU_ALPHA = 1.702
SWIGLU_LIMIT = 7.0
RMS_EPS = 1e-6

V7X_VMEM_LIMIT_BYTES = 52 * 1024 * 1024
LANES = 128
QPAD = 256

NEG_INF = float("-inf")
LOG2_E = 1.4426950408889634
SB_UNDERFLOW_LOG = -104.0


def _cparams(*sem):
    return pltpu.CompilerParams(dimension_semantics=sem, vmem_limit_bytes=V7X_VMEM_LIMIT_BYTES)


def _split_bf16(a):
    hi = a.astype(BF16)
    lo = (a - hi.astype(F32)).astype(BF16)
    return hi, lo


def _dot(a, b):
    return jnp.dot(a, b, preferred_element_type=F32)


def _dot_nt(a, b):
    return lax.dot_general(a, b, (((1,), (1,)), ((), ())), preferred_element_type=F32)


def _rms(x, g):
    ms = jnp.mean(x * x, axis=-1, keepdims=True)
    return x * lax.rsqrt(ms + RMS_EPS) * g


def _norm_mod(x, g, shift, scale):
    return _rms(x, g) * (1.0 + scale) + shift


def _adaln_kernel(c_ref, w_ref, b_ref, o_ref):
    c = c_ref[...]
    s = c * (1.0 / (1.0 + jnp.exp(-c)))
    s_hi, s_lo = _split_bf16(s)
    w_hi, w_lo = _split_bf16(w_ref[0])
    o_ref[0] = _dot(s_hi, w_hi) + _dot(s_lo, w_hi) + _dot(s_hi, w_lo) + b_ref[0]


def _adaln(c, w, b):
    L, D, N = w.shape
    B = c.shape[0]
    tn = 1024
    return pl.pallas_call(
        _adaln_kernel,
        grid=(L, N // tn),
        in_specs=[pl.BlockSpec((B, D), lambda l, j: (0, 0)),
                  pl.BlockSpec((1, D, tn), lambda l, j: (l, 0, j)),
                  pl.BlockSpec((1, 1, tn), lambda l, j: (l, 0, j))],
        out_specs=pl.BlockSpec((1, B, tn), lambda l, j: (l, 0, j)),
        out_shape=jax.ShapeDtypeStruct((L, B, N), F32),
        compiler_params=_cparams("parallel", "parallel"),
        name="adaln",
    )(c, w, b.reshape(L, 1, N))


def _cossin_kernel(a_ref, c_ref, s_ref):
    a = a_ref[...]
    c_ref[...] = jnp.cos(a)
    s_ref[...] = jnp.sin(a)


def _cossin(ang):
    R = ang.shape[0]
    tr = min(R, 512)
    spec = pl.BlockSpec((tr, LANES), lambda i: (i, 0))
    return pl.pallas_call(
        _cossin_kernel, grid=(R // tr,), in_specs=[spec], out_specs=[spec, spec],
        out_shape=[jax.ShapeDtypeStruct(ang.shape, F32)] * 2,
        compiler_params=_cparams("parallel"), name="cossin",
    )(ang)


def _nm_matmul_kernel(x_ref, g_ref, sh_ref, sc_ref, w_ref, o_ref, h_ref, *, n_scaled, col_scale):
    j = pl.program_id(1)

    @pl.when(j == 0)
    def _():
        h_ref[...] = _norm_mod(x_ref[...], g_ref[...], sh_ref[0], sc_ref[0]).astype(BF16)

    acc = _dot(h_ref[...], w_ref[...])
    s = jnp.where(j < n_scaled, col_scale, 1.0).astype(F32)
    o_ref[...] = (acc * s).astype(o_ref.dtype)


def _nm_matmul(x, g, shift, scale, w, S, *, q_cols=0, q_scale=1.0):
    T, D = x.shape
    N = w.shape[1]
    tm = min(S, 512)
    tn = 1024
    per_b = S // tm
    B = T // S
    mod_spec = pl.BlockSpec((1, 1, D), lambda i, j: (i // per_b, 0, 0))
    return pl.pallas_call(
        functools.partial(_nm_matmul_kernel, n_scaled=q_cols // tn, col_scale=q_scale),
        grid=(T // tm, N // tn),
        in_specs=[pl.BlockSpec((tm, D), lambda i, j: (i, 0)),
                  pl.BlockSpec((1, D), lambda i, j: (0, 0)),
                  mod_spec, mod_spec,
                  pl.BlockSpec((D, tn), lambda i, j: (0, j))],
        out_specs=pl.BlockSpec((tm, tn), lambda i, j: (i, j)),
        out_shape=jax.ShapeDtypeStruct((T, N), BF16),
        scratch_shapes=[pltpu.VMEM((tm, D), BF16)],
        compiler_params=_cparams("parallel", "arbitrary"),
        name="norm_mod_matmul",
    )(x, g.reshape(1, D), shift.reshape(B, 1, D), scale.reshape(B, 1, D), w)


def _sb_attn_kernel(q_ref, k_ref, v_ref, tri_ref, o_ref, acc_ref, run_ref, *, tq, tk, heads):
    qi = pl.program_id(2)
    Dh = SB_HEAD_DIM
    tri = tri_ref[...]
    n_kb = ((qi + 1) * tq) // tk
    n_diag = tq // tk
    t_loc = lax.broadcasted_iota(I32, (tq, tk), 0)
    s_loc = lax.broadcasted_iota(I32, (tq, tk), 1)
    acc_ref[...] = jnp.zeros_like(acc_ref)
    run_ref[...] = jnp.zeros_like(run_ref)

    def block(j, mask):
        ks = pl.multiple_of(j * tk, tk)
        heads_cols = [slice(g * Dh, (g + 1) * Dh) for g in range(heads)]
        zs = [_dot_nt(q_ref[:, c], k_ref[pl.ds(ks, tk), c]) for c in heads_cols]
        ps = []
        for z in zs:
            lk = jnp.minimum(-z, 0.0) - jnp.log(1.0 + jnp.exp(-jnp.abs(z)))
            if mask is not None:
                lk = jnp.where(mask, lk, 0.0)
            lk_hi, lk_lo = _split_bf16(lk)
            ps.append(_dot(lk_hi, tri) + _dot(lk_lo, tri))
        worst = None
        for g in range(heads):
            run = run_ref[g]
            a = jnp.exp(zs[g] + ps[g][:, :tk] + run)
            if mask is not None:
                a = jnp.where(mask, a, 0.0)
            acc_ref[g] += _dot(a.astype(BF16), v_ref[pl.ds(ks, tk), heads_cols[g]])
            run = run + ps[g][:, tk:]
            run_ref[g] = run
            m = jnp.max(run)
            worst = m if worst is None else jnp.maximum(worst, m)
        return worst

    for d in range(n_diag):
        j = n_kb - 1 - d
        worst = block(j, (j * tk - qi * tq + s_loc) < t_loc)

    def cond(c):
        j, worst = c
        return jnp.logical_and(j >= 0, worst > SB_UNDERFLOW_LOG)

    def body(c):
        j, _ = c
        return j - 1, block(j, None)

    lax.while_loop(cond, body, (n_kb - n_diag - 1, worst))
    for g in range(heads):
        o_ref[:, g * Dh:(g + 1) * Dh] = acc_ref[g].astype(o_ref.dtype)


def _sb_attention(qkv, B, S):
    H, Dh = SB_HEADS, SB_HEAD_DIM
    T = qkv.shape[0]
    tk = 128
    tq = min(S, 128)
    G = 8
    nq = S // tq
    hb = H // G
    jj = jnp.arange(tk)[:, None]
    ss = jnp.arange(tk + LANES)[None, :]
    tri = ((jj >= ss) | (ss >= tk)).astype(BF16)
    return pl.pallas_call(
        functools.partial(_sb_attn_kernel, tq=tq, tk=tk, heads=G),
        grid=(B, hb, nq),
        in_specs=[pl.BlockSpec((tq, G * Dh), lambda b, h, i: (b * nq + i, h)),
                  pl.BlockSpec((S, G * Dh), lambda b, h, i: (b, hb + h)),
                  pl.BlockSpec((S, G * Dh), lambda b, h, i: (b, 2 * hb + h)),
                  pl.BlockSpec((tk, tk + LANES), lambda b, h, i: (0, 0))],
        out_specs=pl.BlockSpec((tq, G * Dh), lambda b, h, i: (b * nq + i, h)),
        out_shape=jax.ShapeDtypeStruct((T, H * Dh), BF16),
        scratch_shapes=[pltpu.VMEM((G, tq, Dh), F32), pltpu.VMEM((G, tq, LANES), F32)],
        compiler_params=_cparams("parallel", "parallel", "arbitrary"),
        name="sb_attention",
    )(qkv, qkv, qkv, tri)


def _oproj_kernel(o_ref, w_ref, x_ref, ga_ref, g_ref, sh_ref, sc_ref, rwt_ref, rb_ref,
                  xo_ref, h_ref, lg_ref):
    mix = _dot(o_ref[...], w_ref[...])
    xn = x_ref[...] + ga_ref[0] * mix
    xo_ref[...] = xn
    h = _norm_mod(xn, g_ref[...], sh_ref[0], sc_ref[0])
    h_ref[...] = h
    h_hi, h_lo = _split_bf16(h)
    r_hi, r_lo = _split_bf16(rwt_ref[...])
    lg_ref[...] = _dot_nt(r_hi, h_hi) + _dot_nt(r_lo, h_hi) + _dot_nt(r_hi, h_lo) + rb_ref[...]


def _oproj(o, w, x, gate, g, shift, scale, router_w, router_b, S):
    T, D = x.shape
    E = router_w.shape[1]
    B = T // S
    tm = min(S, 512)
    per_b = S // tm
    row = pl.BlockSpec((tm, D), lambda i: (i, 0))
    mod = pl.BlockSpec((1, 1, D), lambda i: (i // per_b, 0, 0))
    return pl.pallas_call(
        _oproj_kernel,
        grid=(T // tm,),
        in_specs=[row,
                  pl.BlockSpec((D, D), lambda i: (0, 0), pipeline_mode=pl.Buffered(1)),
                  row, mod,
                  pl.BlockSpec((1, D), lambda i: (0, 0)),
                  mod, mod,
                  pl.BlockSpec((E, D), lambda i: (0, 0)),
                  pl.BlockSpec((E, 1), lambda i: (0, 0))],
        out_specs=[row, row, pl.BlockSpec((E, tm), lambda i: (0, i))],
        out_shape=[jax.ShapeDtypeStruct((T, D), F32),
                   jax.ShapeDtypeStruct((T, D), F32),
                   jax.ShapeDtypeStruct((E, T), F32)],
        compiler_params=_cparams("parallel"),
        name="oproj_residual_router",
    )(o, w, x, gate.reshape(B, 1, D), g.reshape(1, D), shift.reshape(B, 1, D),
      scale.reshape(B, 1, D), router_w.T, router_b.reshape(E, 1))


def _route_kernel(lg_ref, tri_ref, idx_ref, gate_ref, rank_ref, cnt_ref, carry_ref):
    i = pl.program_id(0)

    @pl.when(i == 0)
    def _():
        carry_ref[...] = jnp.zeros_like(carry_ref)

    work = lg_ref[...]
    E, tt = work.shape
    eio = lax.broadcasted_iota(I32, (E, tt), 0)
    vals, onehots = [], []
    for k in range(TOP_K):
        m = jnp.max(work, axis=0, keepdims=True)
        ix = jnp.min(jnp.where(work == m, eio, E), axis=0, keepdims=True)
        oh = eio == ix
        work = jnp.where(oh, NEG_INF, work)
        vals.append(m)
        onehots.append(oh)
        idx_ref[k:k + 1, :] = ix
    es = [jnp.exp(v - vals[0]) for v in vals]
    den = es[0] + es[1] + es[2] + es[3]
    for k in range(TOP_K):
        gate_ref[k:k + 1, :] = es[k] / den
    sel = jnp.zeros((E, tt), F32)
    for oh in onehots:
        sel = sel + jnp.where(oh, 1.0, 0.0)
    cnt = _dot(sel.astype(BF16), tri_ref[...]) + carry_ref[:, 0:1]
    for k in range(TOP_K):
        r = jnp.sum(jnp.where(onehots[k], cnt - 1.0, 0.0), axis=0, keepdims=True)
        rank_ref[k:k + 1, :] = r.astype(I32)
    last = jnp.broadcast_to(cnt[:, tt - 1:tt], (E, LANES))
    carry_ref[...] = last
    cnt_ref[...] = last


def _route(logits_t):
    E, T = logits_t.shape
    tt = min(T, 512)
    tri = (jnp.arange(tt)[:, None] <= jnp.arange(tt)[None, :]).astype(BF16)
    tok = pl.BlockSpec((TOP_K, tt), lambda i: (0, i))
    return pl.pallas_call(
        _route_kernel,
        grid=(T // tt,),
        in_specs=[pl.BlockSpec((E, tt), lambda i: (0, i)),
                  pl.BlockSpec((tt, tt), lambda i: (0, 0))],
        out_specs=[tok, tok, tok, pl.BlockSpec((E, LANES), lambda i: (0, 0))],
        out_shape=[jax.ShapeDtypeStruct((TOP_K, T), I32),
                   jax.ShapeDtypeStruct((TOP_K, T), F32),
                   jax.ShapeDtypeStruct((TOP_K, T), I32),
                   jax.ShapeDtypeStruct((E, LANES), F32)],
        scratch_shapes=[pltpu.VMEM((E, LANES), F32)],
        compiler_params=_cparams("arbitrary"),
        name="route_topk_rank",
    )(logits_t, tri)


def _dest_kernel(idx_ref, rank_ref, ps_ref, dest_ref):
    E = ps_ref.shape[0]
    tt = idx_ref.shape[1]
    eio = lax.broadcasted_iota(I32, (E, tt), 0)
    ps = ps_ref[...]
    for k in range(TOP_K):
        oh = eio == idx_ref[k:k + 1, :]
        base = jnp.sum(jnp.where(oh, ps, 0), axis=0, keepdims=True)
        dest_ref[k:k + 1, :] = base + rank_ref[k:k + 1, :]


def _dest(idx_t, rank_t, pstart):
    T = idx_t.shape[1]
    E = pstart.shape[0]
    tt = min(T, 2048)
    tok = pl.BlockSpec((TOP_K, tt), lambda i: (0, i))
    return pl.pallas_call(
        _dest_kernel, grid=(T // tt,),
        in_specs=[tok, tok, pl.BlockSpec((E, 1), lambda i: (0, 0))],
        out_specs=tok,
        out_shape=jax.ShapeDtypeStruct((TOP_K, T), I32),
        compiler_params=_cparams("parallel"), name="route_dest",
    )(idx_t, rank_t, pstart.reshape(E, 1))


def _row_copy(src, s_row, dst, d_row, sem):
    return pltpu.make_async_copy(src.at[pl.ds(s_row, 1)], dst.at[pl.ds(d_row, 1)], sem)


def _dispatch_kernel(pend_ref, padded_ref, dest_ref, h_ref, xs_hbm, zbuf, sem, zsem, *, tt, tm):
    @pl.when(pl.program_id(0) == 0)
    def _():
        zbuf[...] = jnp.zeros_like(zbuf)

        def zero_copy(e):
            start = pl.multiple_of(pend_ref[e] - tm, tm)
            return pltpu.make_async_copy(zbuf, xs_hbm.at[pl.ds(start, tm)], zsem)

        def start(e, c):
            @pl.when(padded_ref[e] > 0)
            def _():
                zero_copy(e).start()
            return c

        def wait(e, c):
            @pl.when(padded_ref[e] > 0)
            def _():
                zero_copy(e).wait()
            return c

        lax.fori_loop(0, N_EXPERTS, start, 0)
        lax.fori_loop(0, N_EXPERTS, wait, 0)

        def tail_copy(b):
            return pltpu.make_async_copy(zbuf, xs_hbm.at[pl.ds(pl.multiple_of(b * tm, tm), tm)], zsem)

        def tail_start(b, c):
            tail_copy(b).start()
            return c

        def tail_wait(b, c):
            tail_copy(b).wait()
            return c

        first_unused = pend_ref[N_EXPERTS - 1] // tm
        lax.fori_loop(first_unused, xs_hbm.shape[0] // tm, tail_start, 0)
        lax.fori_loop(first_unused, xs_hbm.shape[0] // tm, tail_wait, 0)

    def issue(r, c):
        for k in range(TOP_K):
            _row_copy(h_ref, r, xs_hbm, dest_ref[r * TOP_K + k], sem).start()
        return c

    lax.fori_loop(0, tt, issue, 0)

    def drain(r, c):
        for k in range(TOP_K):
            _row_copy(h_ref, 0, xs_hbm, 0, sem).wait()
        return c

    lax.fori_loop(0, tt, drain, 0)


def _dispatch(pends, padded, dest_flat, h, n_rows, tm):
    T, D = h.shape
    tt = min(T, 256)
    return pl.pallas_call(
        functools.partial(_dispatch_kernel, tt=tt, tm=tm),
        grid_spec=pltpu.PrefetchScalarGridSpec(
            num_scalar_prefetch=2,
            grid=(T // tt,),
            in_specs=[pl.BlockSpec((tt * TOP_K,), lambda i, pe, pa: (i,), memory_space=pltpu.SMEM),
                      pl.BlockSpec((tt, D), lambda i, pe, pa: (i, 0))],
            out_specs=pl.BlockSpec(memory_space=pl.ANY),
            scratch_shapes=[pltpu.VMEM((tm, D), h.dtype), pltpu.SemaphoreType.DMA(()),
                            pltpu.SemaphoreType.DMA(())],
        ),
        out_shape=jax.ShapeDtypeStruct((n_rows, D), h.dtype),
        compiler_params=_cparams("arbitrary"),
        name="moe_dispatch",
    )(pends, padded, dest_flat, h)


def _gmm_kernel(be_ref, nu_ref, xs_ref, wgu_ref, bgu_ref, wd_ref, bd_ref, y_ref):
    del be_ref
    i = pl.program_id(0)
    F = D_FF_EXPERT

    @pl.when(i < nu_ref[0])
    def _():
        x = xs_ref[...].astype(BF16)
        gu = _dot(x, wgu_ref[0, 0]) + bgu_ref[0, 0]
        g = jnp.minimum(gu[:, :F], SWIGLU_LIMIT)
        u = jnp.clip(gu[:, F:], -SWIGLU_LIMIT, SWIGLU_LIMIT)
        a = g * (1.0 / (1.0 + jnp.exp(-SWIGLU_ALPHA * g))) * (u + 1.0)
        y_ref[...] = _dot(a.astype(BF16), wd_ref[0, 0]) + bd_ref[0, 0]

    @pl.when(i >= nu_ref[0])
    def _():
        y_ref[...] = jnp.zeros_like(y_ref)


def _gmm(block_expert, n_used, xs, w_gu, b_gu, w_down, b_down, layer, tm):
    n_rows, D = xs.shape
    L, E, _, F2 = w_gu.shape
    F = F2 // 2
    n_blocks = n_rows // tm

    def row_map(i, be, nu):
        return (jnp.minimum(i, nu[0] - 1), 0)

    def w_map(i, be, nu):
        return (layer, be[i], 0, 0)

    return pl.pallas_call(
        _gmm_kernel,
        grid_spec=pltpu.PrefetchScalarGridSpec(
            num_scalar_prefetch=2,
            grid=(n_blocks,),
            in_specs=[pl.BlockSpec((tm, D), row_map),
                      pl.BlockSpec((1, 1, D, F2), w_map),
                      pl.BlockSpec((1, 1, 1, F2), w_map),
                      pl.BlockSpec((1, 1, F, D), w_map),
                      pl.BlockSpec((1, 1, 1, D), w_map)],
            out_specs=pl.BlockSpec((tm, D), lambda i, be, nu: (i, 0)),
        ),
        out_shape=jax.ShapeDtypeStruct((n_rows, D), F32),
        compiler_params=_cparams("arbitrary"),
        name="moe_expert_mlp",
    )(block_expert, n_used, xs, w_gu, b_gu.reshape(L, E, 1, F2), w_down, b_down.reshape(L, E, 1, D))


def _combine_kernel(dest_ref, gates_ref, x_ref, gm_ref, gf_ref, y_hbm, o_ref, ybuf, sem, *, tt, final):
    def issue(r, c):
        for k in range(TOP_K):
            _row_copy(y_hbm, dest_ref[r * TOP_K + k], ybuf.at[k], r, sem).start()
        return c

    lax.fori_loop(0, tt, issue, 0)

    def drain(r, c):
        for k in range(TOP_K):
            _row_copy(y_hbm, 0, ybuf.at[k], 0, sem).wait()
        return c

    lax.fori_loop(0, tt, drain, 0)

    g = gates_ref[...]
    moe = g[:, 0:1] * ybuf[0]
    for k in range(1, TOP_K):
        moe = moe + g[:, k:k + 1] * ybuf[k]
    xn = x_ref[...] + gm_ref[0] * moe
    if final:
        xn = _rms(xn, gf_ref[...])
    o_ref[...] = xn


def _combine(dest_flat, gates, x, gate_m, g_final, y, S, final):
    T, D = x.shape
    B = T // S
    tt = min(S, 128)
    per_b = S // tt
    row = pl.BlockSpec((tt, D), lambda i: (i, 0))
    return pl.pallas_call(
        functools.partial(_combine_kernel, tt=tt, final=final),
        grid=(T // tt,),
        in_specs=[pl.BlockSpec((tt * TOP_K,), lambda i: (i,), memory_space=pltpu.SMEM),
                  pl.BlockSpec((tt, TOP_K), lambda i: (i, 0)),
                  row,
                  pl.BlockSpec((1, 1, D), lambda i: (i // per_b, 0, 0)),
                  pl.BlockSpec((1, D), lambda i: (0, 0)),
                  pl.BlockSpec(memory_space=pl.ANY)],
        out_specs=row,
        out_shape=jax.ShapeDtypeStruct((T, D), F32),
        scratch_shapes=[pltpu.VMEM((TOP_K, tt, D), F32), pltpu.SemaphoreType.DMA(())],
        compiler_params=_cparams("arbitrary"),
        name="moe_combine",
    )(dest_flat, gates, x, gate_m.reshape(B, 1, D), g_final.reshape(1, D), y)


def _moe(h, logits_t, x, gate_m, g_final, w_gu, b_gu, w_down, b_down, layer, S, final):
    T, D = h.shape
    E = N_EXPERTS
    tm = 512
    idx_t, gates_t, rank_t, cnt = _route(logits_t)
    counts = cnt[:, 0].astype(I32)
    padded = (counts + tm - 1) // tm * tm
    pends = jnp.cumsum(padded)
    pstart = pends - padded
    n_rows = T * TOP_K + E * tm
    n_blocks = n_rows // tm
    block_start = jnp.arange(n_blocks, dtype=I32) * tm
    block_expert = jnp.minimum(jnp.sum((pends[None, :] <= block_start[:, None]).astype(I32), axis=1), E - 1)
    n_used = (pends[-1:] // tm).astype(I32)
    dest_flat = _dest(idx_t, rank_t, pstart).T.reshape(-1)
    xs = _dispatch(pends, padded, dest_flat, h, n_rows, tm)
    y = _gmm(block_expert, n_used, xs, w_gu, b_gu, w_down, b_down, layer, tm)
    return _combine(dest_flat, gates_t.T, x, gate_m, g_final, y, S, final)


def _dq_kernel(x_ref, g_ref, sh_ref, sc_ref, w_ref, gcq_ref, o_ref):
    h = _norm_mod(x_ref[...], g_ref[...], sh_ref[0], sc_ref[0]).astype(BF16)
    o_ref[...] = _rms(_dot(h, w_ref[...]), gcq_ref[...]).astype(o_ref.dtype)


def _dq(x, g, shift, scale, w, gcq, S):
    T, D = x.shape
    R = w.shape[1]
    B = T // S
    tm = min(S, 512)
    per_b = S // tm
    mod = pl.BlockSpec((1, 1, D), lambda i: (i // per_b, 0, 0))
    return pl.pallas_call(
        _dq_kernel, grid=(T // tm,),
        in_specs=[pl.BlockSpec((tm, D), lambda i: (i, 0)),
                  pl.BlockSpec((1, D), lambda i: (0, 0)), mod, mod,
                  pl.BlockSpec((D, R), lambda i: (0, 0)),
                  pl.BlockSpec((1, R), lambda i: (0, 0))],
        out_specs=pl.BlockSpec((tm, R), lambda i: (i, 0)),
        out_shape=jax.ShapeDtypeStruct((T, R), BF16),
        compiler_params=_cparams("parallel"), name="q_down_norm",
    )(x, g.reshape(1, D), shift.reshape(B, 1, D), scale.reshape(B, 1, D), w, gcq.reshape(1, R))


def _uq_rope_kernel(cq_ref, w_ref, cos_ref, sin_ref, o_ref):
    cq = cq_ref[...]
    cos = cos_ref[...]
    sin = sin_ref[...]
    for h in range(w_ref.shape[0]):
        acc = _dot(cq, w_ref[h])
        q = acc * cos + pltpu.roll(acc, QPAD - QK_ROPE_DIM, 1) * sin
        o_ref[:, h * QPAD:(h + 1) * QPAD] = q.astype(o_ref.dtype)


def _uq_rope(cq, wq, cos_p, sin_p):
    T, R = cq.shape
    H = wq.shape[0]
    tm = min(T, 512)
    tab = pl.BlockSpec((tm, QPAD), lambda i: (i, 0))
    return pl.pallas_call(
        _uq_rope_kernel, grid=(T // tm,),
        in_specs=[pl.BlockSpec((tm, R), lambda i: (i, 0)),
                  pl.BlockSpec((H, R, QPAD), lambda i: (0, 0, 0)), tab, tab],
        out_specs=pl.BlockSpec((tm, H * QPAD), lambda i: (i, 0)),
        out_shape=jax.ShapeDtypeStruct((T, H * QPAD), BF16),
        compiler_params=_cparams("parallel"), name="q_up_rope",
    )(cq, wq, cos_p, sin_p)


def _kv_latent_kernel(x_ref, g_ref, sh_ref, sc_ref, w_ref, gc_ref, cos_ref, sin_ref, ckv_ref, kr_ref):
    h = _norm_mod(x_ref[...], g_ref[...], sh_ref[0], sc_ref[0]).astype(BF16)
    lat = _dot(h, w_ref[...])
    ckv_ref[...] = _rms(lat[:, :KV_LORA_RANK], gc_ref[...]).astype(ckv_ref.dtype)
    t = lat[:, KV_LORA_RANK:]
    kr_ref[...] = (t * cos_ref[...] + pltpu.roll(t, QK_ROPE_DIM, 1) * sin_ref[...]).astype(kr_ref.dtype)


def _kv_latent(x, g, shift, scale, w_ext, g_ckv, cos_k, sin_k, S):
    T, D = x.shape
    B = T // S
    N = w_ext.shape[1]
    tm = min(S, 512)
    per_b = S // tm
    mod = pl.BlockSpec((1, 1, D), lambda i: (i // per_b, 0, 0))
    tab = pl.BlockSpec((tm, LANES), lambda i: (i, 0))
    return pl.pallas_call(
        _kv_latent_kernel, grid=(T // tm,),
        in_specs=[pl.BlockSpec((tm, D), lambda i: (i, 0)),
                  pl.BlockSpec((1, D), lambda i: (0, 0)), mod, mod,
                  pl.BlockSpec((D, N), lambda i: (0, 0)),
                  pl.BlockSpec((1, KV_LORA_RANK), lambda i: (0, 0)), tab, tab],
        out_specs=[pl.BlockSpec((tm, KV_LORA_RANK), lambda i: (i, 0)), tab],
        out_shape=[jax.ShapeDtypeStruct((T, KV_LORA_RANK), BF16),
                   jax.ShapeDtypeStruct((T, LANES), BF16)],
        compiler_params=_cparams("parallel"), name="kv_latent",
    )(x, g.reshape(1, D), shift.reshape(B, 1, D), scale.reshape(B, 1, D), w_ext,
      g_ckv.reshape(1, KV_LORA_RANK), cos_k, sin_k)


def _kv_up_kernel(c_ref, wk_ref, wvt_ref, kr_ref, k_ref, vt_ref, *, tk):
    c = c_ref[...]
    kn = _dot(c, wk_ref[...]).astype(k_ref.dtype)
    kr = kr_ref[...]
    for h in range(MLA_HEADS):
        k_ref[:, h * QPAD:h * QPAD + QK_NOPE_DIM] = kn[:, h * QK_NOPE_DIM:(h + 1) * QK_NOPE_DIM]
        k_ref[:, h * QPAD + QK_NOPE_DIM:(h + 1) * QPAD] = kr
    vt = _dot_nt(wvt_ref[...], c).astype(vt_ref.dtype)
    for j in range(vt_ref.shape[0]):
        vt_ref[j] = vt[:, j * tk:(j + 1) * tk]


def _kv_up(ckv, w_uk, w_uv_t, k_rope, tk):
    T, R = ckv.shape
    H = MLA_HEADS
    tm = min(T, 512)
    return pl.pallas_call(
        functools.partial(_kv_up_kernel, tk=tk), grid=(T // tm,),
        in_specs=[pl.BlockSpec((tm, R), lambda i: (i, 0)),
                  pl.BlockSpec((R, H * QK_NOPE_DIM), lambda i: (0, 0)),
                  pl.BlockSpec((H * V_HEAD_DIM, R), lambda i: (0, 0)),
                  pl.BlockSpec((tm, LANES), lambda i: (i, 0))],
        out_specs=[pl.BlockSpec((tm, H * QPAD), lambda i: (i, 0)),
                   pl.BlockSpec((tm // tk, H * V_HEAD_DIM, tk), lambda i: (i, 0, 0))],
        out_shape=[jax.ShapeDtypeStruct((T, H * QPAD), BF16),
                   jax.ShapeDtypeStruct((T // tk, H * V_HEAD_DIM, tk), BF16)],
        compiler_params=_cparams("parallel"), name="kv_up",
    )(ckv, w_uk, w_uv_t, k_rope)


def _mla_attn_kernel(q_ref, k_ref, vt_ref, o_ref, m_ref, l_ref, acc_ref, s_ref, *, tq, heads):
    qi = pl.program_id(2)
    Dv = V_HEAD_DIM
    m_ref[...] = jnp.full_like(m_ref, NEG_INF)
    l_ref[...] = jnp.zeros_like(l_ref)
    acc_ref[...] = jnp.zeros_like(acc_ref)

    def scores(j):
        ks = pl.multiple_of(j * tq, tq)
        return [_dot_nt(k_ref[pl.ds(ks, tq), g * QPAD:(g + 1) * QPAD], q_ref[:, g * QPAD:(g + 1) * QPAD])
                for g in range(heads)]

    def softmax_pv(j, sts, masked):
        for g in range(heads):
            st = sts[g]
            if masked:
                kc = lax.broadcasted_iota(I32, (tq, tq), 0) // CHUNK
                qc = lax.broadcasted_iota(I32, (tq, tq), 1) // CHUNK
                st = jnp.where(kc <= qc, st, NEG_INF)
            m_prev = m_ref[g]
            m_new = jnp.maximum(m_prev, jnp.max(st, axis=0, keepdims=True))
            alpha = jnp.exp2(m_prev - m_new)
            p = jnp.exp2(st - m_new)
            l_ref[g] = alpha * l_ref[g] + jnp.sum(p, axis=0, keepdims=True)
            acc_ref[g] = alpha * acc_ref[g] + _dot(vt_ref[j, g * Dv:(g + 1) * Dv, :], p.astype(BF16))
            m_ref[g] = m_new

    for g, st in enumerate(scores(0)):
        s_ref[g] = st

    def body(j, c):
        cur = [s_ref[g] for g in range(heads)]
        nxt = scores(j + 1)
        softmax_pv(j, cur, False)
        for g in range(heads):
            s_ref[g] = nxt[g]
        return c

    lax.fori_loop(0, qi, body, 0)
    softmax_pv(qi, [s_ref[g] for g in range(heads)], True)
    for g in range(heads):
        o_ref[:, g * Dv:(g + 1) * Dv] = (acc_ref[g] / l_ref[g]).T.astype(o_ref.dtype)


def _mla_attention(q_cat, k_cat, v_t, B, S, tq):
    H = MLA_HEADS
    T = q_cat.shape[0]
    G = 4
    nq = S // tq
    return pl.pallas_call(
        functools.partial(_mla_attn_kernel, tq=tq, heads=G),
        grid=(B, H // G, nq),
        in_specs=[pl.BlockSpec((tq, G * QPAD), lambda b, h, i: (b * nq + i, h)),
                  pl.BlockSpec((S, G * QPAD), lambda b, h, i: (b, h)),
                  pl.BlockSpec((nq, G * V_HEAD_DIM, tq), lambda b, h, i: (b, h, 0))],
        out_specs=pl.BlockSpec((tq, G * V_HEAD_DIM), lambda b, h, i: (b * nq + i, h)),
        out_shape=jax.ShapeDtypeStruct((T, H * V_HEAD_DIM), BF16),
        scratch_shapes=[pltpu.VMEM((G, 1, tq), F32), pltpu.VMEM((G, 1, tq), F32),
                        pltpu.VMEM((G, V_HEAD_DIM, tq), F32), pltpu.VMEM((G, tq, tq), F32)],
        compiler_params=_cparams("parallel", "parallel", "arbitrary"),
        name="mla_attention",
    )(q_cat, k_cat, v_t)


def _rot_cols(w):
    half = QK_ROPE_DIM // 2
    return jnp.concatenate([-w[..., half:], w[..., :half]], axis=-1)


def _q_weights(w_uq):
    R = w_uq.shape[0]
    w = w_uq.reshape(R, MLA_HEADS, QK_NOPE_DIM + QK_ROPE_DIM)
    wn, wr = w[..., :QK_NOPE_DIM], w[..., QK_NOPE_DIM:]
    return jnp.concatenate([wn, wr, _rot_cols(wr)], axis=-1).transpose(1, 0, 2).astype(BF16)


def kernel(x, c, positions, w_mod, b_mod, g_mix, g_ffn, w_qkv_a, w_o_a, g_kv_in, w_mod_kv, b_mod_kv,
           w_dkv, g_ckv, w_uk, w_uv, w_dq_b, g_cq_b, w_uq_b, w_o_b, router_w, router_b, w_gu, b_gu,
           w_down, b_down, g_final):
    B, S, D = x.shape
    T = B * S
    depth = w_mod.shape[0]
    n_a = w_qkv_a.shape[0]
    xt = x.reshape(T, D)

    mod = _adaln(c, w_mod, b_mod)
    mod_kv = _adaln(c, w_mod_kv[None], b_mod_kv[None])[0]

    half = QK_ROPE_DIM // 2
    inv_freq = ROPE_BASE ** (-jnp.arange(half, dtype=F32) / half)
    ang = positions.astype(F32).reshape(T, 1) * inv_freq
    cos_f, sin_f = _cossin(ang.reshape(T * half // LANES, LANES))
    cos = cos_f.reshape(T, half)
    sin = sin_f.reshape(T, half)
    z64 = jnp.zeros((T, QK_ROPE_DIM), F32)
    cos_k = jnp.concatenate([cos, cos, z64], axis=1)
    sin_k = jnp.concatenate([sin, sin, z64], axis=1)
    qs = LOG2_E * (QK_NOPE_DIM + QK_ROPE_DIM) ** -0.5
    cos_q = jnp.concatenate([jnp.ones((T, QK_NOPE_DIM), F32), cos_k], axis=1) * qs
    sin_q = jnp.concatenate([jnp.zeros((T, QK_NOPE_DIM), F32), sin_k], axis=1) * qs

    tq_mla = min(S, 256)
    w_gu_b = w_gu.astype(BF16)
    w_down_b = w_down.astype(BF16)

    shared = None
    for layer in range(depth):
        m = mod[layer]
        shift_a, scale_a, gate_a, shift_m, scale_m, gate_m = [m[:, i * D:(i + 1) * D] for i in range(6)]
        if layer < n_a:
            qkv = _nm_matmul(xt, g_mix[layer], shift_a, scale_a, w_qkv_a[layer].astype(BF16), S,
                             q_cols=SB_HEADS * SB_HEAD_DIM, q_scale=SB_HEAD_DIM ** -0.5)
            o = _sb_attention(qkv, B, S)
            w_o = w_o_a[layer]
        else:
            i = layer - n_a
            k_cat, v_b = shared
            cq = _dq(xt, g_mix[layer], shift_a, scale_a, w_dq_b[i].astype(BF16), g_cq_b[i], S)
            q_cat = _uq_rope(cq, _q_weights(w_uq_b[i]), cos_q, sin_q)
            o = _mla_attention(q_cat, k_cat, v_b, B, S, tq_mla)
            w_o = w_o_b[i]
        xt, h, logits_t = _oproj(o, w_o.astype(BF16), xt, gate_a, g_ffn[layer], shift_m, scale_m,
                                 router_w[layer], router_b[layer], S)
        xt = _moe(h, logits_t, xt, gate_m, g_final, w_gu_b, b_gu, w_down_b, b_down, layer, S,
                  final=(layer == depth - 1))
        if layer == n_a - 1:
            w_ext = jnp.concatenate([w_dkv, _rot_cols(w_dkv[:, KV_LORA_RANK:])], axis=1).astype(BF16)
            ckv, k_rope = _kv_latent(xt, g_kv_in, mod_kv[:, :D], mod_kv[:, D:], w_ext, g_ckv,
                                     cos_k, sin_k, S)
            shared = _kv_up(ckv, w_uk.astype(BF16), w_uv.T.astype(BF16), k_rope, tq_mla)
    return xt.reshape(B, S, D)
```

```python
import functools

import jax
import jax.numpy as jnp
from jax import lax
from jax.experimental import pallas as pl
from jax.experimental.pallas import tpu as pltpu

F32 = jnp.float32
BF16 = jnp.bfloat16
I32 = jnp.int32
U32 = jnp.uint32

CHUNK = 64
SB_HEADS = 16
SB_HEAD_DIM = 128
MLA_HEADS = 16
QK_NOPE_DIM = 128
QK_ROPE_DIM = 64
V_HEAD_DIM = 128
KV_LORA_RANK = 512
ROPE_BASE = 10000.0
N_EXPERTS = 32
TOP_K = 4
D_FF_EXPERT = 768
SWIGLU_ALPHA = 1.702
SWIGLU_LIMIT = 7.0
RMS_EPS = 1e-6

V7X_VMEM_LIMIT_BYTES = 52 * 1024 * 1024
LANES = 128
QPAD = 256
VT_ROWS = V_HEAD_DIM + 16

NEG_INF = float("-inf")
LOG2_E = 1.4426950408889634
SB_UNDERFLOW_LOG = -104.0


def _cparams(*sem):
    return pltpu.CompilerParams(dimension_semantics=sem, vmem_limit_bytes=V7X_VMEM_LIMIT_BYTES)


def _split_bf16(a):
    hi = a.astype(BF16)
    lo = (a - hi.astype(F32)).astype(BF16)
    return hi, lo


def _dot(a, b):
    return jnp.dot(a, b, preferred_element_type=F32)


def _dot_nt(a, b):
    return lax.dot_general(a, b, (((1,), (1,)), ((), ())), preferred_element_type=F32)


def _pack_bf16_pair(lo, hi):
    lo_bits = lax.bitcast_convert_type(lo.astype(BF16).astype(F32), U32) >> 16
    hi_bits = lax.bitcast_convert_type(hi.astype(BF16).astype(F32), U32) & jnp.uint32(0xFFFF0000)
    return hi_bits | lo_bits


def _unpack_bf16_pair(w):
    lo = lax.bitcast_convert_type(w << 16, F32)
    hi = lax.bitcast_convert_type(w & jnp.uint32(0xFFFF0000), F32)
    return lo, hi


def _rms(x, g):
    ms = jnp.mean(x * x, axis=-1, keepdims=True)
    return x * lax.rsqrt(ms + RMS_EPS) * g


def _norm_mod(x, g, shift, scale):
    return _rms(x, g) * (1.0 + scale) + shift


def _adaln_kernel(c_ref, w_ref, b_ref, o_ref):
    c = c_ref[...]
    s = c * (1.0 / (1.0 + jnp.exp(-c)))
    s_hi, s_lo = _split_bf16(s)
    w_hi, w_lo = _split_bf16(w_ref[0])
    o_ref[0] = _dot(s_hi, w_hi) + _dot(s_lo, w_hi) + _dot(s_hi, w_lo) + b_ref[0]


def _adaln(c, w, b):
    L, D, N = w.shape
    B = c.shape[0]
    tn = 1024
    return pl.pallas_call(
        _adaln_kernel,
        grid=(L, N // tn),
        in_specs=[pl.BlockSpec((B, D), lambda l, j: (0, 0)),
                  pl.BlockSpec((1, D, tn), lambda l, j: (l, 0, j)),
                  pl.BlockSpec((1, 1, tn), lambda l, j: (l, 0, j))],
        out_specs=pl.BlockSpec((1, B, tn), lambda l, j: (l, 0, j)),
        out_shape=jax.ShapeDtypeStruct((L, B, N), F32),
        compiler_params=_cparams("parallel", "parallel"),
        name="adaln",
    )(c, w, b.reshape(L, 1, N))


def _cossin_kernel(a_ref, c_ref, s_ref):
    a = a_ref[...]
    c_ref[...] = jnp.cos(a)
    s_ref[...] = jnp.sin(a)


def _cossin(ang):
    R = ang.shape[0]
    tr = min(R, 512)
    spec = pl.BlockSpec((tr, LANES), lambda i: (i, 0))
    return pl.pallas_call(
        _cossin_kernel, grid=(R // tr,), in_specs=[spec], out_specs=[spec, spec],
        out_shape=[jax.ShapeDtypeStruct(ang.shape, F32)] * 2,
        compiler_params=_cparams("parallel"), name="cossin",
    )(ang)


def _nm_matmul_kernel(x_ref, g_ref, sh_ref, sc_ref, w_ref, o_ref, h_ref, *, n_scaled, col_scale):
    j = pl.program_id(1)

    @pl.when(j == 0)
    def _():
        h_ref[...] = _norm_mod(x_ref[...], g_ref[...], sh_ref[0], sc_ref[0]).astype(BF16)

    acc = _dot(h_ref[...], w_ref[...])
    s = jnp.where(j < n_scaled, col_scale, 1.0).astype(F32)
    o_ref[...] = (acc * s).astype(o_ref.dtype)


def _nm_matmul(x, g, shift, scale, w, S, *, q_cols=0, q_scale=1.0):
    T, D = x.shape
    N = w.shape[1]
    tm = min(S, 512)
    tn = 1024
    per_b = S // tm
    B = T // S
    mod_spec = pl.BlockSpec((1, 1, D), lambda i, j: (i // per_b, 0, 0))
    return pl.pallas_call(
        functools.partial(_nm_matmul_kernel, n_scaled=q_cols // tn, col_scale=q_scale),
        grid=(T // tm, N // tn),
        in_specs=[pl.BlockSpec((tm, D), lambda i, j: (i, 0)),
                  pl.BlockSpec((1, D), lambda i, j: (0, 0)),
                  mod_spec, mod_spec,
                  pl.BlockSpec((D, tn), lambda i, j: (0, j))],
        out_specs=pl.BlockSpec((tm, tn), lambda i, j: (i, j)),
        out_shape=jax.ShapeDtypeStruct((T, N), BF16),
        scratch_shapes=[pltpu.VMEM((tm, D), BF16)],
        compiler_params=_cparams("parallel", "arbitrary"),
        name="norm_mod_matmul",
    )(x, g.reshape(1, D), shift.reshape(B, 1, D), scale.reshape(B, 1, D), w)


def _sb_attn_kernel(q_ref, k_ref, v_ref, tri_ref, o_ref, acc_ref, run_ref, *, tq, tk, heads):
    qi = pl.program_id(2)
    Dh = SB_HEAD_DIM
    tri = tri_ref[...]
    n_kb = ((qi + 1) * tq) // tk
    n_diag = tq // tk
    t_loc = lax.broadcasted_iota(I32, (tq, tk), 0)
    s_loc = lax.broadcasted_iota(I32, (tq, tk), 1)
    acc_ref[...] = jnp.zeros_like(acc_ref)
    run_ref[...] = jnp.zeros_like(run_ref)

    def block(j, mask):
        ks = pl.multiple_of(j * tk, tk)
        heads_cols = [slice(g * Dh, (g + 1) * Dh) for g in range(heads)]
        zs = [_dot_nt(q_ref[:, c], k_ref[pl.ds(ks, tk), c]) for c in heads_cols]
        ps = []
        for z in zs:
            lk = jnp.minimum(-z, 0.0) - jnp.log(1.0 + jnp.exp(-jnp.abs(z)))
            if mask is not None:
                lk = jnp.where(mask, lk, 0.0)
            lk_hi, lk_lo = _split_bf16(lk)
            ps.append(_dot(lk_hi, tri) + _dot(lk_lo, tri))
        worst = None
        for g in range(heads):
            run = run_ref[g]
            a = jnp.exp(zs[g] + ps[g][:, :tk] + run)
            if mask is not None:
                a = jnp.where(mask, a, 0.0)
            acc_ref[g] += _dot(a.astype(BF16), v_ref[pl.ds(ks, tk), heads_cols[g]])
            run = run + ps[g][:, tk:]
            run_ref[g] = run
            m = jnp.max(run)
            worst = m if worst is None else jnp.maximum(worst, m)
        return worst

    for d in range(n_diag):
        j = n_kb - 1 - d
        worst = block(j, (j * tk - qi * tq + s_loc) < t_loc)

    def cond(c):
        j, worst = c
        return jnp.logical_and(j >= 0, worst > SB_UNDERFLOW_LOG)

    def body(c):
        j, _ = c
        return j - 1, block(j, None)

    lax.while_loop(cond, body, (n_kb - n_diag - 1, worst))
    for g in range(heads):
        o_ref[:, g * Dh:(g + 1) * Dh] = acc_ref[g].astype(o_ref.dtype)


def _sb_attention(qkv, B, S):
    H, Dh = SB_HEADS, SB_HEAD_DIM
    T = qkv.shape[0]
    tk = 128
    tq = min(S, 128)
    G = 8
    nq = S // tq
    hb = H // G
    jj = jnp.arange(tk)[:, None]
    ss = jnp.arange(tk + LANES)[None, :]
    tri = ((jj >= ss) | (ss >= tk)).astype(BF16)
    return pl.pallas_call(
        functools.partial(_sb_attn_kernel, tq=tq, tk=tk, heads=G),
        grid=(B, hb, nq),
        in_specs=[pl.BlockSpec((tq, G * Dh), lambda b, h, i: (b * nq + i, h)),
                  pl.BlockSpec((S, G * Dh), lambda b, h, i: (b, hb + h)),
                  pl.BlockSpec((S, G * Dh), lambda b, h, i: (b, 2 * hb + h)),
                  pl.BlockSpec((tk, tk + LANES), lambda b, h, i: (0, 0))],
        out_specs=pl.BlockSpec((tq, G * Dh), lambda b, h, i: (b * nq + i, h)),
        out_shape=jax.ShapeDtypeStruct((T, H * Dh), BF16),
        scratch_shapes=[pltpu.VMEM((G, tq, Dh), F32), pltpu.VMEM((G, tq, LANES), F32)],
        compiler_params=_cparams("parallel", "parallel", "arbitrary"),
        name="sb_attention",
    )(qkv, qkv, qkv, tri)


def _oproj_kernel(o_ref, w_ref, x_ref, ga_ref, g_ref, sh_ref, sc_ref, rwt_ref, rb_ref,
                  xo_ref, h_ref, lg_ref):
    mix = _dot(o_ref[...], w_ref[...])
    xn = x_ref[...] + ga_ref[0] * mix
    xo_ref[...] = xn
    h = _norm_mod(xn, g_ref[...], sh_ref[0], sc_ref[0])
    half = h.shape[1] // 2
    h_ref[...] = _pack_bf16_pair(h[:, :half], h[:, half:])
    h_hi, h_lo = _split_bf16(h)
    r_hi, r_lo = _split_bf16(rwt_ref[...])
    lg_ref[...] = _dot_nt(r_hi, h_hi) + _dot_nt(r_lo, h_hi) + _dot_nt(r_hi, h_lo) + rb_ref[...]


def _oproj(o, w, x, gate, g, shift, scale, router_w, router_b, S):
    T, D = x.shape
    E = router_w.shape[1]
    B = T // S
    tm = min(S, 512)
    per_b = S // tm
    row = pl.BlockSpec((tm, D), lambda i: (i, 0))
    mod = pl.BlockSpec((1, 1, D), lambda i: (i // per_b, 0, 0))
    return pl.pallas_call(
        _oproj_kernel,
        grid=(T // tm,),
        in_specs=[row,
                  pl.BlockSpec((D, D), lambda i: (0, 0), pipeline_mode=pl.Buffered(1)),
                  row, mod,
                  pl.BlockSpec((1, D), lambda i: (0, 0)),
                  mod, mod,
                  pl.BlockSpec((E, D), lambda i: (0, 0)),
                  pl.BlockSpec((E, 1), lambda i: (0, 0))],
        out_specs=[row, pl.BlockSpec((tm, D // 2), lambda i: (i, 0)), pl.BlockSpec((E, tm), lambda i: (0, i))],
        out_shape=[jax.ShapeDtypeStruct((T, D), F32),
                   jax.ShapeDtypeStruct((T, D // 2), U32),
                   jax.ShapeDtypeStruct((E, T), F32)],
        compiler_params=_cparams("parallel"),
        name="oproj_residual_router",
    )(o, w, x, gate.reshape(B, 1, D), g.reshape(1, D), shift.reshape(B, 1, D),
      scale.reshape(B, 1, D), router_w.T, router_b.reshape(E, 1))


def _route_kernel(lg_ref, tri_ref, idx_ref, gate_ref, rank_ref, cnt_ref, carry_ref):
    i = pl.program_id(0)

    @pl.when(i == 0)
    def _():
        carry_ref[...] = jnp.zeros_like(carry_ref)

    work = lg_ref[...]
    E, tt = work.shape
    eio = lax.broadcasted_iota(I32, (E, tt), 0)
    vals, onehots = [], []
    for k in range(TOP_K):
        m = jnp.max(work, axis=0, keepdims=True)
        ix = jnp.min(jnp.where(work == m, eio, E), axis=0, keepdims=True)
        oh = eio == ix
        work = jnp.where(oh, NEG_INF, work)
        vals.append(m)
        onehots.append(oh)
        idx_ref[k:k + 1, :] = ix
    es = [jnp.exp(v - vals[0]) for v in vals]
    den = es[0] + es[1] + es[2] + es[3]
    for k in range(TOP_K):
        gate_ref[k:k + 1, :] = es[k] / den
    sel = jnp.zeros((E, tt), F32)
    for oh in onehots:
        sel = sel + jnp.where(oh, 1.0, 0.0)
    cnt = _dot(sel.astype(BF16), tri_ref[...]) + carry_ref[:, 0:1]
    for k in range(TOP_K):
        r = jnp.sum(jnp.where(onehots[k], cnt - 1.0, 0.0), axis=0, keepdims=True)
        rank_ref[k:k + 1, :] = r.astype(I32)
    last = jnp.broadcast_to(cnt[:, tt - 1:tt], (E, LANES))
    carry_ref[...] = last
    cnt_ref[...] = last


def _route(logits_t):
    E, T = logits_t.shape
    tt = min(T, 512)
    tri = (jnp.arange(tt)[:, None] <= jnp.arange(tt)[None, :]).astype(BF16)
    tok = pl.BlockSpec((TOP_K, tt), lambda i: (0, i))
    return pl.pallas_call(
        _route_kernel,
        grid=(T // tt,),
        in_specs=[pl.BlockSpec((E, tt), lambda i: (0, i)),
                  pl.BlockSpec((tt, tt), lambda i: (0, 0))],
        out_specs=[tok, tok, tok, pl.BlockSpec((E, LANES), lambda i: (0, 0))],
        out_shape=[jax.ShapeDtypeStruct((TOP_K, T), I32),
                   jax.ShapeDtypeStruct((TOP_K, T), F32),
                   jax.ShapeDtypeStruct((TOP_K, T), I32),
                   jax.ShapeDtypeStruct((E, LANES), F32)],
        scratch_shapes=[pltpu.VMEM((E, LANES), F32)],
        compiler_params=_cparams("arbitrary"),
        name="route_topk_rank",
    )(logits_t, tri)


def _dest_kernel(idx_ref, rank_ref, ps_ref, dest_ref):
    E = ps_ref.shape[0]
    tt = idx_ref.shape[1]
    eio = lax.broadcasted_iota(I32, (E, tt), 0)
    ps = ps_ref[...]
    for k in range(TOP_K):
        oh = eio == idx_ref[k:k + 1, :]
        base = jnp.sum(jnp.where(oh, ps, 0), axis=0, keepdims=True)
        dest_ref[k:k + 1, :] = base + rank_ref[k:k + 1, :]


def _dest(idx_t, rank_t, pstart):
    T = idx_t.shape[1]
    E = pstart.shape[0]
    tt = min(T, 2048)
    tok = pl.BlockSpec((TOP_K, tt), lambda i: (0, i))
    return pl.pallas_call(
        _dest_kernel, grid=(T // tt,),
        in_specs=[tok, tok, pl.BlockSpec((E, 1), lambda i: (0, 0))],
        out_specs=tok,
        out_shape=jax.ShapeDtypeStruct((TOP_K, T), I32),
        compiler_params=_cparams("parallel"), name="route_dest",
    )(idx_t, rank_t, pstart.reshape(E, 1))


def _row_copy(src, s_row, dst, d_row, sem):
    return pltpu.make_async_copy(src.at[pl.ds(s_row, 1)], dst.at[pl.ds(d_row, 1)], sem)


def _dispatch_kernel(pend_ref, padded_ref, dest_ref, h_ref, xs_hbm, zbuf, sem, zsem, *, tt, tm):
    @pl.when(pl.program_id(0) == 0)
    def _():
        zbuf[...] = jnp.zeros_like(zbuf)

        def zero_copy(e):
            start = pl.multiple_of(pend_ref[e] - tm, tm)
            return pltpu.make_async_copy(zbuf, xs_hbm.at[pl.ds(start, tm)], zsem)

        def start(e, c):
            @pl.when(padded_ref[e] > 0)
            def _():
                zero_copy(e).start()
            return c

        def wait(e, c):
            @pl.when(padded_ref[e] > 0)
            def _():
                zero_copy(e).wait()
            return c

        lax.fori_loop(0, N_EXPERTS, start, 0)
        lax.fori_loop(0, N_EXPERTS, wait, 0)

        def tail_copy(b):
            return pltpu.make_async_copy(zbuf, xs_hbm.at[pl.ds(pl.multiple_of(b * tm, tm), tm)], zsem)

        def tail_start(b, c):
            tail_copy(b).start()
            return c

        def tail_wait(b, c):
            tail_copy(b).wait()
            return c

        first_unused = pend_ref[N_EXPERTS - 1] // tm
        lax.fori_loop(first_unused, xs_hbm.shape[0] // tm, tail_start, 0)
        lax.fori_loop(first_unused, xs_hbm.shape[0] // tm, tail_wait, 0)

    def issue(r, c):
        for k in range(TOP_K):
            _row_copy(h_ref, r, xs_hbm, dest_ref[r * TOP_K + k], sem).start()
        return c

    lax.fori_loop(0, tt, issue, 0)

    def drain(r, c):
        for k in range(TOP_K):
            _row_copy(h_ref, 0, xs_hbm, 0, sem).wait()
        return c

    lax.fori_loop(0, tt, drain, 0)


def _dispatch(pends, padded, dest_flat, h, n_rows, tm):
    T, D = h.shape
    tt = min(T, 256)
    return pl.pallas_call(
        functools.partial(_dispatch_kernel, tt=tt, tm=tm),
        grid_spec=pltpu.PrefetchScalarGridSpec(
            num_scalar_prefetch=2,
            grid=(T // tt,),
            in_specs=[pl.BlockSpec((tt * TOP_K,), lambda i, pe, pa: (i,), memory_space=pltpu.SMEM),
                      pl.BlockSpec((tt, D), lambda i, pe, pa: (i, 0))],
            out_specs=pl.BlockSpec(memory_space=pl.ANY),
            scratch_shapes=[pltpu.VMEM((tm, D), h.dtype), pltpu.SemaphoreType.DMA(()),
                            pltpu.SemaphoreType.DMA(())],
        ),
        out_shape=jax.ShapeDtypeStruct((n_rows, D), h.dtype),
        compiler_params=_cparams("arbitrary"),
        name="moe_dispatch",
    )(pends, padded, dest_flat, h)


def _gmm_kernel(be_ref, nu_ref, xs_ref, wgu_ref, bgu_ref, wd_ref, bd_ref, y_ref):
    del be_ref
    i = pl.program_id(0)
    F = D_FF_EXPERT

    @pl.when(i < nu_ref[0])
    def _():
        x_lo, x_hi = _unpack_bf16_pair(xs_ref[...])
        half = x_lo.shape[1]
        gu = (_dot(x_lo.astype(BF16), wgu_ref[0, 0, :half, :])
              + _dot(x_hi.astype(BF16), wgu_ref[0, 0, half:, :]) + bgu_ref[0, 0])
        g = jnp.minimum(gu[:, :F], SWIGLU_LIMIT)
        u = jnp.clip(gu[:, F:], -SWIGLU_LIMIT, SWIGLU_LIMIT)
        a = g * (1.0 / (1.0 + jnp.exp(-SWIGLU_ALPHA * g))) * (u + 1.0)
        y = _dot(a.astype(BF16), wd_ref[0, 0]) + bd_ref[0, 0]
        y_ref[...] = _pack_bf16_pair(y[:, :half], y[:, half:])

    @pl.when(i >= nu_ref[0])
    def _():
        y_ref[...] = jnp.zeros_like(y_ref)


def _gmm(block_expert, n_used, xs, w_gu, b_gu, w_down, b_down, layer, tm):
    n_rows, Dp = xs.shape
    L, E, D, F2 = w_gu.shape
    F = F2 // 2
    n_blocks = n_rows // tm

    def row_map(i, be, nu):
        return (jnp.minimum(i, nu[0] - 1), 0)

    def w_map(i, be, nu):
        return (layer, be[i], 0, 0)

    return pl.pallas_call(
        _gmm_kernel,
        grid_spec=pltpu.PrefetchScalarGridSpec(
            num_scalar_prefetch=2,
            grid=(n_blocks,),
            in_specs=[pl.BlockSpec((tm, Dp), row_map),
                      pl.BlockSpec((1, 1, D, F2), w_map),
                      pl.BlockSpec((1, 1, 1, F2), w_map),
                      pl.BlockSpec((1, 1, F, D), w_map),
                      pl.BlockSpec((1, 1, 1, D), w_map)],
            out_specs=pl.BlockSpec((tm, Dp), lambda i, be, nu: (i, 0)),
        ),
        out_shape=jax.ShapeDtypeStruct((n_rows, Dp), U32),
        compiler_params=_cparams("arbitrary"),
        name="moe_expert_mlp",
    )(block_expert, n_used, xs, w_gu, b_gu.reshape(L, E, 1, F2), w_down, b_down.reshape(L, E, 1, D))


def _combine_kernel(dest_ref, dest_next_ref, gates_ref, x_ref, gm_ref, gf_ref, y_hbm, o_ref, ybuf, sems,
                    *, tt, final):
    i = pl.program_id(0)
    slot = lax.rem(i, 2)

    def gather(dref, s):
        def issue(r, c):
            for k in range(TOP_K):
                _row_copy(y_hbm, dref[r * TOP_K + k], ybuf.at[s, k], r, sems.at[s]).start()
            return c

        lax.fori_loop(0, tt, issue, 0)

    @pl.when(i == 0)
    def _():
        gather(dest_ref, 0)

    @pl.when(i + 1 < pl.num_programs(0))
    def _():
        gather(dest_next_ref, 1 - slot)

    def drain(r, c):
        for k in range(TOP_K):
            _row_copy(y_hbm, 0, ybuf.at[slot, k], 0, sems.at[slot]).wait()
        return c

    lax.fori_loop(0, tt, drain, 0)

    g = gates_ref[...]
    half = ybuf.shape[-1]
    moe_lo = moe_hi = None
    for k in range(TOP_K):
        lo, hi = _unpack_bf16_pair(ybuf[slot, k])
        gk = g[:, k:k + 1]
        moe_lo = gk * lo if moe_lo is None else moe_lo + gk * lo
        moe_hi = gk * hi if moe_hi is None else moe_hi + gk * hi
    gm = gm_ref[0]
    x_lo = x_ref[:, :half] + gm[:, :half] * moe_lo
    x_hi = x_ref[:, half:] + gm[:, half:] * moe_hi
    if final:
        ms = (jnp.sum(x_lo * x_lo, axis=-1, keepdims=True)
              + jnp.sum(x_hi * x_hi, axis=-1, keepdims=True)) / (2 * half)
        inv = lax.rsqrt(ms + RMS_EPS)
        x_lo = x_lo * inv * gf_ref[:, :half]
        x_hi = x_hi * inv * gf_ref[:, half:]
    o_ref[:, :half] = x_lo
    o_ref[:, half:] = x_hi


def _combine(dest_flat, gates, x, gate_m, g_final, y, S, final):
    T, D = x.shape
    B = T // S
    tt = min(S, 256)
    per_b = S // tt
    n_tiles = T // tt
    row = pl.BlockSpec((tt, D), lambda i: (i, 0))
    return pl.pallas_call(
        functools.partial(_combine_kernel, tt=tt, final=final),
        grid=(n_tiles,),
        in_specs=[pl.BlockSpec((tt * TOP_K,), lambda i: (i,), memory_space=pltpu.SMEM),
                  pl.BlockSpec((tt * TOP_K,), lambda i: (jnp.minimum(i + 1, n_tiles - 1),),
                               memory_space=pltpu.SMEM),
                  pl.BlockSpec((tt, TOP_K), lambda i: (i, 0)),
                  row,
                  pl.BlockSpec((1, 1, D), lambda i: (i // per_b, 0, 0)),
                  pl.BlockSpec((1, D), lambda i: (0, 0)),
                  pl.BlockSpec(memory_space=pl.ANY)],
        out_specs=row,
        out_shape=jax.ShapeDtypeStruct((T, D), F32),
        scratch_shapes=[pltpu.VMEM((2, TOP_K, tt, D // 2), U32), pltpu.SemaphoreType.DMA((2,))],
        compiler_params=_cparams("arbitrary"),
        name="moe_combine",
    )(dest_flat, dest_flat, gates, x, gate_m.reshape(B, 1, D), g_final.reshape(1, D), y)


def _moe(h, logits_t, x, gate_m, g_final, w_gu, b_gu, w_down, b_down, layer, S, final):
    T, D = h.shape
    E = N_EXPERTS
    tm = 512
    idx_t, gates_t, rank_t, cnt = _route(logits_t)
    counts = cnt[:, 0].astype(I32)
    padded = (counts + tm - 1) // tm * tm
    pends = jnp.cumsum(padded)
    pstart = pends - padded
    n_rows = T * TOP_K + E * tm
    n_blocks = n_rows // tm
    block_start = jnp.arange(n_blocks, dtype=I32) * tm
    block_expert = jnp.minimum(jnp.sum((pends[None, :] <= block_start[:, None]).astype(I32), axis=1), E - 1)
    n_used = (pends[-1:] // tm).astype(I32)
    dest_flat = _dest(idx_t, rank_t, pstart).T.reshape(-1)
    xs = _dispatch(pends, padded, dest_flat, h, n_rows, tm)
    y = _gmm(block_expert, n_used, xs, w_gu, b_gu, w_down, b_down, layer, tm)
    return _combine(dest_flat, gates_t.T, x, gate_m, g_final, y, S, final)


def _dq_kernel(x_ref, g_ref, sh_ref, sc_ref, w_ref, gcq_ref, o_ref):
    h = _norm_mod(x_ref[...], g_ref[...], sh_ref[0], sc_ref[0]).astype(BF16)
    o_ref[...] = _rms(_dot(h, w_ref[...]), gcq_ref[...]).astype(o_ref.dtype)


def _dq(x, g, shift, scale, w, gcq, S):
    T, D = x.shape
    R = w.shape[1]
    B = T // S
    tm = min(S, 512)
    per_b = S // tm
    mod = pl.BlockSpec((1, 1, D), lambda i: (i // per_b, 0, 0))
    return pl.pallas_call(
        _dq_kernel, grid=(T // tm,),
        in_specs=[pl.BlockSpec((tm, D), lambda i: (i, 0)),
                  pl.BlockSpec((1, D), lambda i: (0, 0)), mod, mod,
                  pl.BlockSpec((D, R), lambda i: (0, 0)),
                  pl.BlockSpec((1, R), lambda i: (0, 0))],
        out_specs=pl.BlockSpec((tm, R), lambda i: (i, 0)),
        out_shape=jax.ShapeDtypeStruct((T, R), BF16),
        compiler_params=_cparams("parallel"), name="q_down_norm",
    )(x, g.reshape(1, D), shift.reshape(B, 1, D), scale.reshape(B, 1, D), w, gcq.reshape(1, R))


def _uq_rope_kernel(cq_ref, w_ref, cos_ref, sin_ref, o_ref):
    cq = cq_ref[...]
    cos = cos_ref[...]
    sin = sin_ref[...]
    for h in range(w_ref.shape[0]):
        acc = _dot(cq, w_ref[h])
        q = acc * cos + pltpu.roll(acc, QPAD - QK_ROPE_DIM, 1) * sin
        o_ref[:, h * QPAD:(h + 1) * QPAD] = q.astype(o_ref.dtype)


def _uq_rope(cq, wq, cos_p, sin_p):
    T, R = cq.shape
    H = wq.shape[0]
    tm = min(T, 512)
    tab = pl.BlockSpec((tm, QPAD), lambda i: (i, 0))
    return pl.pallas_call(
        _uq_rope_kernel, grid=(T // tm,),
        in_specs=[pl.BlockSpec((tm, R), lambda i: (i, 0)),
                  pl.BlockSpec((H, R, QPAD), lambda i: (0, 0, 0)), tab, tab],
        out_specs=pl.BlockSpec((tm, H * QPAD), lambda i: (i, 0)),
        out_shape=jax.ShapeDtypeStruct((T, H * QPAD), BF16),
        compiler_params=_cparams("parallel"), name="q_up_rope",
    )(cq, wq, cos_p, sin_p)


def _kv_latent_kernel(x_ref, g_ref, sh_ref, sc_ref, w_ref, gc_ref, cos_ref, sin_ref, ckv_ref, kr_ref):
    h = _norm_mod(x_ref[...], g_ref[...], sh_ref[0], sc_ref[0]).astype(BF16)
    lat = _dot(h, w_ref[...])
    ckv_ref[...] = _rms(lat[:, :KV_LORA_RANK], gc_ref[...]).astype(ckv_ref.dtype)
    t = lat[:, KV_LORA_RANK:]
    kr_ref[...] = (t * cos_ref[...] + pltpu.roll(t, QK_ROPE_DIM, 1) * sin_ref[...]).astype(kr_ref.dtype)


def _kv_latent(x, g, shift, scale, w_ext, g_ckv, cos_k, sin_k, S):
    T, D = x.shape
    B = T // S
    N = w_ext.shape[1]
    tm = min(S, 512)
    per_b = S // tm
    mod = pl.BlockSpec((1, 1, D), lambda i: (i // per_b, 0, 0))
    tab = pl.BlockSpec((tm, LANES), lambda i: (i, 0))
    return pl.pallas_call(
        _kv_latent_kernel, grid=(T // tm,),
        in_specs=[pl.BlockSpec((tm, D), lambda i: (i, 0)),
                  pl.BlockSpec((1, D), lambda i: (0, 0)), mod, mod,
                  pl.BlockSpec((D, N), lambda i: (0, 0)),
                  pl.BlockSpec((1, KV_LORA_RANK), lambda i: (0, 0)), tab, tab],
        out_specs=[pl.BlockSpec((tm, KV_LORA_RANK), lambda i: (i, 0)), tab],
        out_shape=[jax.ShapeDtypeStruct((T, KV_LORA_RANK), BF16),
                   jax.ShapeDtypeStruct((T, LANES), BF16)],
        compiler_params=_cparams("parallel"), name="kv_latent",
    )(x, g.reshape(1, D), shift.reshape(B, 1, D), scale.reshape(B, 1, D), w_ext,
      g_ckv.reshape(1, KV_LORA_RANK), cos_k, sin_k)


def _kv_up_kernel(c_ref, wk_ref, wvt_ref, kr_ref, k_ref, vt_ref, *, tk):
    c = c_ref[...]
    kn = _dot(c, wk_ref[...]).astype(k_ref.dtype)
    kr = kr_ref[...]
    for h in range(MLA_HEADS):
        k_ref[:, h * QPAD:h * QPAD + QK_NOPE_DIM] = kn[:, h * QK_NOPE_DIM:(h + 1) * QK_NOPE_DIM]
        k_ref[:, h * QPAD + QK_NOPE_DIM:(h + 1) * QPAD] = kr
    vt = _dot_nt(wvt_ref[...], c).astype(vt_ref.dtype)
    ones = jnp.ones((VT_ROWS - V_HEAD_DIM, tk), vt_ref.dtype)
    for j in range(vt_ref.shape[0]):
        for h in range(MLA_HEADS):
            vt_ref[j, h * VT_ROWS:h * VT_ROWS + V_HEAD_DIM, :] = vt[h * V_HEAD_DIM:(h + 1) * V_HEAD_DIM,
                                                                    j * tk:(j + 1) * tk]
            vt_ref[j, h * VT_ROWS + V_HEAD_DIM:(h + 1) * VT_ROWS, :] = ones


def _kv_up(ckv, w_uk, w_uv_t, k_rope, tk):
    T, R = ckv.shape
    H = MLA_HEADS
    tm = min(T, 512)
    return pl.pallas_call(
        functools.partial(_kv_up_kernel, tk=tk), grid=(T // tm,),
        in_specs=[pl.BlockSpec((tm, R), lambda i: (i, 0)),
                  pl.BlockSpec((R, H * QK_NOPE_DIM), lambda i: (0, 0)),
                  pl.BlockSpec((H * V_HEAD_DIM, R), lambda i: (0, 0)),
                  pl.BlockSpec((tm, LANES), lambda i: (i, 0))],
        out_specs=[pl.BlockSpec((tm, H * QPAD), lambda i: (i, 0)),
                   pl.BlockSpec((tm // tk, H * VT_ROWS, tk), lambda i: (i, 0, 0))],
        out_shape=[jax.ShapeDtypeStruct((T, H * QPAD), BF16),
                   jax.ShapeDtypeStruct((T // tk, H * VT_ROWS, tk), BF16)],
        compiler_params=_cparams("parallel"), name="kv_up",
    )(ckv, w_uk, w_uv_t, k_rope)


def _mla_attn_kernel(q_ref, k_ref, vt_ref, o_ref, m_ref, acc_ref, s_ref, *, tq, heads):
    qi = pl.program_id(2)
    Dv = V_HEAD_DIM
    m_ref[...] = jnp.full_like(m_ref, NEG_INF)
    acc_ref[...] = jnp.zeros_like(acc_ref)

    def scores(j):
        ks = pl.multiple_of(j * tq, tq)
        return [_dot_nt(k_ref[pl.ds(ks, tq), g * QPAD:(g + 1) * QPAD], q_ref[:, g * QPAD:(g + 1) * QPAD])
                for g in range(heads)]

    def softmax_pv(j, sts, masked):
        for g in range(heads):
            st = sts[g]
            if masked:
                kc = lax.broadcasted_iota(I32, (tq, tq), 0) // CHUNK
                qc = lax.broadcasted_iota(I32, (tq, tq), 1) // CHUNK
                st = jnp.where(kc <= qc, st, NEG_INF)
            m_prev = m_ref[g]
            m_new = jnp.maximum(m_prev, jnp.max(st, axis=0, keepdims=True))
            alpha = jnp.exp2(m_prev - m_new)
            p = jnp.exp2(st - m_new)
            acc_ref[g] = alpha * acc_ref[g] + _dot(vt_ref[j, g * VT_ROWS:(g + 1) * VT_ROWS, :], p.astype(BF16))
            m_ref[g] = m_new

    for g, st in enumerate(scores(0)):
        s_ref[g] = st

    def body(j, c):
        cur = [s_ref[g] for g in range(heads)]
        nxt = scores(j + 1)
        softmax_pv(j, cur, False)
        for g in range(heads):
            s_ref[g] = nxt[g]
        return c

    lax.fori_loop(0, qi, body, 0)
    softmax_pv(qi, [s_ref[g] for g in range(heads)], True)
    for g in range(heads):
        out_t = acc_ref[g, :Dv, :] / acc_ref[g, Dv:Dv + 1, :]
        o_ref[:, g * Dv:(g + 1) * Dv] = out_t.T.astype(o_ref.dtype)


def _mla_attention(q_cat, k_cat, v_t, B, S, tq):
    H = MLA_HEADS
    T = q_cat.shape[0]
    G = 4
    nq = S // tq
    return pl.pallas_call(
        functools.partial(_mla_attn_kernel, tq=tq, heads=G),
        grid=(B, H // G, nq),
        in_specs=[pl.BlockSpec((tq, G * QPAD), lambda b, h, i: (b * nq + i, h)),
                  pl.BlockSpec((S, G * QPAD), lambda b, h, i: (b, h)),
                  pl.BlockSpec((nq, G * VT_ROWS, tq), lambda b, h, i: (b, h, 0))],
        out_specs=pl.BlockSpec((tq, G * V_HEAD_DIM), lambda b, h, i: (b * nq + i, h)),
        out_shape=jax.ShapeDtypeStruct((T, H * V_HEAD_DIM), BF16),
        scratch_shapes=[pltpu.VMEM((G, 1, tq), F32),
                        pltpu.VMEM((G, VT_ROWS, tq), F32), pltpu.VMEM((G, tq, tq), F32)],
        compiler_params=_cparams("parallel", "parallel", "arbitrary"),
        name="mla_attention",
    )(q_cat, k_cat, v_t)


def _rot_cols(w):
    half = QK_ROPE_DIM // 2
    return jnp.concatenate([-w[..., half:], w[..., :half]], axis=-1)


def _q_weights(w_uq):
    R = w_uq.shape[0]
    w = w_uq.reshape(R, MLA_HEADS, QK_NOPE_DIM + QK_ROPE_DIM)
    wn, wr = w[..., :QK_NOPE_DIM], w[..., QK_NOPE_DIM:]
    return jnp.concatenate([wn, wr, _rot_cols(wr)], axis=-1).transpose(1, 0, 2).astype(BF16)


def kernel(x, c, positions, w_mod, b_mod, g_mix, g_ffn, w_qkv_a, w_o_a, g_kv_in, w_mod_kv, b_mod_kv,
           w_dkv, g_ckv, w_uk, w_uv, w_dq_b, g_cq_b, w_uq_b, w_o_b, router_w, router_b, w_gu, b_gu,
           w_down, b_down, g_final):
    B, S, D = x.shape
    T = B * S
    depth = w_mod.shape[0]
    n_a = w_qkv_a.shape[0]
    xt = x.reshape(T, D)

    mod = _adaln(c, w_mod, b_mod)
    mod_kv = _adaln(c, w_mod_kv[None], b_mod_kv[None])[0]

    half = QK_ROPE_DIM // 2
    inv_freq = ROPE_BASE ** (-jnp.arange(half, dtype=F32) / half)
    ang = positions.astype(F32).reshape(T, 1) * inv_freq
    cos_f, sin_f = _cossin(ang.reshape(T * half // LANES, LANES))
    cos = cos_f.reshape(T, half)
    sin = sin_f.reshape(T, half)
    z64 = jnp.zeros((T, QK_ROPE_DIM), F32)
    cos_k = jnp.concatenate([cos, cos, z64], axis=1)
    sin_k = jnp.concatenate([sin, sin, z64], axis=1)
    qs = LOG2_E * (QK_NOPE_DIM + QK_ROPE_DIM) ** -0.5
    cos_q = jnp.concatenate([jnp.ones((T, QK_NOPE_DIM), F32), cos_k], axis=1) * qs
    sin_q = jnp.concatenate([jnp.zeros((T, QK_NOPE_DIM), F32), sin_k], axis=1) * qs

    tq_mla = min(S, 256)
    w_gu_b = w_gu.astype(BF16)
    w_down_b = w_down.astype(BF16)

    shared = None
    for layer in range(depth):
        m = mod[layer]
        shift_a, scale_a, gate_a, shift_m, scale_m, gate_m = [m[:, i * D:(i + 1) * D] for i in range(6)]
        if layer < n_a:
            qkv = _nm_matmul(xt, g_mix[layer], shift_a, scale_a, w_qkv_a[layer].astype(BF16), S,
                             q_cols=SB_HEADS * SB_HEAD_DIM, q_scale=SB_HEAD_DIM ** -0.5)
            o = _sb_attention(qkv, B, S)
            w_o = w_o_a[layer]
        else:
            i = layer - n_a
            k_cat, v_b = shared
            cq = _dq(xt, g_mix[layer], shift_a, scale_a, w_dq_b[i].astype(BF16), g_cq_b[i], S)
            q_cat = _uq_rope(cq, _q_weights(w_uq_b[i]), cos_q, sin_q)
            o = _mla_attention(q_cat, k_cat, v_b, B, S, tq_mla)
            w_o = w_o_b[i]
        xt, h, logits_t = _oproj(o, w_o.astype(BF16), xt, gate_a, g_ffn[layer], shift_m, scale_m,
                                 router_w[layer], router_b[layer], S)
        xt = _moe(h, logits_t, xt, gate_m, g_final, w_gu_b, b_gu, w_down_b, b_down, layer, S,
                  final=(layer == depth - 1))
        if layer == n_a - 1:
            w_ext = jnp.concatenate([w_dkv, _rot_cols(w_dkv[:, KV_LORA_RANK:])], axis=1).astype(BF16)
            ckv, k_rope = _kv_latent(xt, g_kv_in, mod_kv[:, :D], mod_kv[:, D:], w_ext, g_ckv,
                                     cos_k, sin_k, S)
            shared = _kv_up(ckv, w_uk.astype(BF16), w_uv.T.astype(BF16), k_rope, tq_mla)
    return xt.reshape(B, S, D)
```

```python
import functools

import jax
import jax.numpy as jnp
from jax import lax
from jax.experimental import pallas as pl
from jax.experimental.pallas import tpu as pltpu

F32 = jnp.float32
BF16 = jnp.bfloat16
I32 = jnp.int32
U32 = jnp.uint32

CHUNK = 64
SB_HEADS = 16
SB_HEAD_DIM = 128
MLA_HEADS = 16
QK_NOPE_DIM = 128
QK_ROPE_DIM = 64
V_HEAD_DIM = 128
KV_LORA_RANK = 512
ROPE_BASE = 10000.0
N_EXPERTS = 32
TOP_K = 4
D_FF_EXPERT = 768
SWIGLU_ALPHA = 1.702
SWIGLU_LIMIT = 7.0
RMS_EPS = 1e-6

V7X_VMEM_LIMIT_BYTES = 52 * 1024 * 1024
LANES = 128
QPAD = 256
VT_ROWS = V_HEAD_DIM + 16

NEG_INF = float("-inf")
LOG2_E = 1.4426950408889634
SB_UNDERFLOW_LOG2 = -150.0


def _cparams(*sem):
    return pltpu.CompilerParams(dimension_semantics=sem, vmem_limit_bytes=V7X_VMEM_LIMIT_BYTES)


def _split_bf16(a):
    hi = a.astype(BF16)
    lo = (a - hi.astype(F32)).astype(BF16)
    return hi, lo


def _dot(a, b):
    return jnp.dot(a, b, preferred_element_type=F32)


def _dot_nt(a, b):
    return lax.dot_general(a, b, (((1,), (1,)), ((), ())), preferred_element_type=F32)


def _pack_bf16_pair(lo, hi):
    lo_bits = lax.bitcast_convert_type(lo.astype(BF16).astype(F32), U32) >> 16
    hi_bits = lax.bitcast_convert_type(hi.astype(BF16).astype(F32), U32) & jnp.uint32(0xFFFF0000)
    return hi_bits | lo_bits


def _unpack_bf16_pair(w):
    lo = lax.bitcast_convert_type(w << 16, F32)
    hi = lax.bitcast_convert_type(w & jnp.uint32(0xFFFF0000), F32)
    return lo, hi


def _rms(x, g):
    ms = jnp.mean(x * x, axis=-1, keepdims=True)
    return x * lax.rsqrt(ms + RMS_EPS) * g


def _norm_mod(x, g, shift, scale):
    return _rms(x, g) * (1.0 + scale) + shift


def _adaln_kernel(c_ref, w_ref, b_ref, o_ref):
    c = c_ref[...]
    s = c * (1.0 / (1.0 + jnp.exp(-c)))
    s_hi, s_lo = _split_bf16(s)
    w_hi, w_lo = _split_bf16(w_ref[0])
    o_ref[0] = _dot(s_hi, w_hi) + _dot(s_lo, w_hi) + _dot(s_hi, w_lo) + b_ref[0]


def _adaln(c, w, b):
    L, D, N = w.shape
    B = c.shape[0]
    tn = 1024
    return pl.pallas_call(
        _adaln_kernel,
        grid=(L, N // tn),
        in_specs=[pl.BlockSpec((B, D), lambda l, j: (0, 0)),
                  pl.BlockSpec((1, D, tn), lambda l, j: (l, 0, j)),
                  pl.BlockSpec((1, 1, tn), lambda l, j: (l, 0, j))],
        out_specs=pl.BlockSpec((1, B, tn), lambda l, j: (l, 0, j)),
        out_shape=jax.ShapeDtypeStruct((L, B, N), F32),
        compiler_params=_cparams("parallel", "parallel"),
        name="adaln",
    )(c, w, b.reshape(L, 1, N))


def _cossin_kernel(a_ref, c_ref, s_ref):
    a = a_ref[...]
    c_ref[...] = jnp.cos(a)
    s_ref[...] = jnp.sin(a)


def _cossin(ang):
    R = ang.shape[0]
    tr = min(R, 512)
    spec = pl.BlockSpec((tr, LANES), lambda i: (i, 0))
    return pl.pallas_call(
        _cossin_kernel, grid=(R // tr,), in_specs=[spec], out_specs=[spec, spec],
        out_shape=[jax.ShapeDtypeStruct(ang.shape, F32)] * 2,
        compiler_params=_cparams("parallel"), name="cossin",
    )(ang)


def _nm_matmul_kernel(x_ref, g_ref, sh_ref, sc_ref, w_ref, o_ref, h_ref, *, n_scaled, col_scale):
    j = pl.program_id(1)

    @pl.when(j == 0)
    def _():
        h_ref[...] = _norm_mod(x_ref[...], g_ref[...], sh_ref[0], sc_ref[0]).astype(BF16)

    acc = _dot(h_ref[...], w_ref[...])
    s = jnp.where(j < n_scaled, col_scale, 1.0).astype(F32)
    o_ref[...] = (acc * s).astype(o_ref.dtype)


def _nm_matmul(x, g, shift, scale, w, S, *, q_cols=0, q_scale=1.0):
    T, D = x.shape
    N = w.shape[1]
    tm = min(S, 512)
    tn = 1024
    per_b = S // tm
    B = T // S
    mod_spec = pl.BlockSpec((1, 1, D), lambda i, j: (i // per_b, 0, 0))
    return pl.pallas_call(
        functools.partial(_nm_matmul_kernel, n_scaled=q_cols // tn, col_scale=q_scale),
        grid=(T // tm, N // tn),
        in_specs=[pl.BlockSpec((tm, D), lambda i, j: (i, 0)),
                  pl.BlockSpec((1, D), lambda i, j: (0, 0)),
                  mod_spec, mod_spec,
                  pl.BlockSpec((D, tn), lambda i, j: (0, j))],
        out_specs=pl.BlockSpec((tm, tn), lambda i, j: (i, j)),
        out_shape=jax.ShapeDtypeStruct((T, N), BF16),
        scratch_shapes=[pltpu.VMEM((tm, D), BF16)],
        compiler_params=_cparams("parallel", "arbitrary"),
        name="norm_mod_matmul",
    )(x, g.reshape(1, D), shift.reshape(B, 1, D), scale.reshape(B, 1, D), w)


def _sb_attn_kernel(q_ref, k_ref, v_ref, tri_ref, o_ref, acc_ref, run_ref, *, tq, tk, heads):
    qi = pl.program_id(2)
    Dh = SB_HEAD_DIM
    tri = tri_ref[...]
    n_kb = ((qi + 1) * tq) // tk
    n_diag = tq // tk
    t_loc = lax.broadcasted_iota(I32, (tq, tk), 0)
    s_loc = lax.broadcasted_iota(I32, (tq, tk), 1)
    acc_ref[...] = jnp.zeros_like(acc_ref)
    run_ref[...] = jnp.zeros_like(run_ref)

    def block(j, mask):
        ks = pl.multiple_of(j * tk, tk)
        heads_cols = [slice(g * Dh, (g + 1) * Dh) for g in range(heads)]
        zs = [_dot_nt(q_ref[:, c], k_ref[pl.ds(ks, tk), c]) for c in heads_cols]
        ps = []
        for z in zs:
            lk = jnp.minimum(-z, 0.0) - jnp.log2(1.0 + jnp.exp2(-jnp.abs(z)))
            if mask is not None:
                lk = jnp.where(mask, lk, 0.0)
            ps.append(_dot(lk.astype(BF16), tri))
        worst = None
        for g in range(heads):
            run = run_ref[g]
            a = jnp.exp2(zs[g] + ps[g][:, :tk] + run)
            if mask is not None:
                a = jnp.where(mask, a, 0.0)
            acc_ref[g] += _dot(a.astype(BF16), v_ref[pl.ds(ks, tk), heads_cols[g]])
            run = run + ps[g][:, tk:]
            run_ref[g] = run
            m = jnp.max(run)
            worst = m if worst is None else jnp.maximum(worst, m)
        return worst

    for d in range(n_diag):
        j = n_kb - 1 - d
        worst = block(j, (j * tk - qi * tq + s_loc) < t_loc)

    def cond(c):
        j, worst = c
        return jnp.logical_and(j >= 0, worst > SB_UNDERFLOW_LOG2)

    def body(c):
        j, _ = c
        return j - 1, block(j, None)

    lax.while_loop(cond, body, (n_kb - n_diag - 1, worst))
    for g in range(heads):
        o_ref[:, g * Dh:(g + 1) * Dh] = acc_ref[g].astype(o_ref.dtype)


def _sb_attention(qkv, B, S):
    H, Dh = SB_HEADS, SB_HEAD_DIM
    T = qkv.shape[0]
    tk = 128
    tq = min(S, 128)
    G = 8
    nq = S // tq
    hb = H // G
    jj = jnp.arange(tk)[:, None]
    ss = jnp.arange(tk + LANES)[None, :]
    tri = ((jj >= ss) | (ss >= tk)).astype(BF16)
    return pl.pallas_call(
        functools.partial(_sb_attn_kernel, tq=tq, tk=tk, heads=G),
        grid=(B, hb, nq),
        in_specs=[pl.BlockSpec((tq, G * Dh), lambda b, h, i: (b * nq + i, h)),
                  pl.BlockSpec((S, G * Dh), lambda b, h, i: (b, hb + h)),
                  pl.BlockSpec((S, G * Dh), lambda b, h, i: (b, 2 * hb + h)),
                  pl.BlockSpec((tk, tk + LANES), lambda b, h, i: (0, 0))],
        out_specs=pl.BlockSpec((tq, G * Dh), lambda b, h, i: (b * nq + i, h)),
        out_shape=jax.ShapeDtypeStruct((T, H * Dh), BF16),
        scratch_shapes=[pltpu.VMEM((G, tq, Dh), F32), pltpu.VMEM((G, tq, LANES), F32)],
        compiler_params=_cparams("parallel", "parallel", "arbitrary"),
        name="sb_attention",
    )(qkv, qkv, qkv, tri)


def _oproj_kernel(o_ref, w_ref, x_ref, ga_ref, g_ref, sh_ref, sc_ref, rwt_ref, rb_ref,
                  xo_ref, h_ref, lg_ref):
    mix = _dot(o_ref[...], w_ref[...])
    xn = x_ref[...] + ga_ref[0] * mix
    xo_ref[...] = xn
    h = _norm_mod(xn, g_ref[...], sh_ref[0], sc_ref[0])
    half = h.shape[1] // 2
    h_ref[...] = _pack_bf16_pair(h[:, :half], h[:, half:])
    h_hi, h_lo = _split_bf16(h)
    r_hi, r_lo = _split_bf16(rwt_ref[...])
    lg_ref[...] = _dot_nt(r_hi, h_hi) + _dot_nt(r_lo, h_hi) + _dot_nt(r_hi, h_lo) + rb_ref[...]


def _oproj(o, w, x, gate, g, shift, scale, router_w, router_b, S):
    T, D = x.shape
    E = router_w.shape[1]
    B = T // S
    tm = min(S, 512)
    per_b = S // tm
    row = pl.BlockSpec((tm, D), lambda i: (i, 0))
    mod = pl.BlockSpec((1, 1, D), lambda i: (i // per_b, 0, 0))
    return pl.pallas_call(
        _oproj_kernel,
        grid=(T // tm,),
        in_specs=[row,
                  pl.BlockSpec((D, D), lambda i: (0, 0), pipeline_mode=pl.Buffered(1)),
                  row, mod,
                  pl.BlockSpec((1, D), lambda i: (0, 0)),
                  mod, mod,
                  pl.BlockSpec((E, D), lambda i: (0, 0)),
                  pl.BlockSpec((E, 1), lambda i: (0, 0))],
        out_specs=[row, pl.BlockSpec((tm, D // 2), lambda i: (i, 0)), pl.BlockSpec((E, tm), lambda i: (0, i))],
        out_shape=[jax.ShapeDtypeStruct((T, D), F32),
                   jax.ShapeDtypeStruct((T, D // 2), U32),
                   jax.ShapeDtypeStruct((E, T), F32)],
        compiler_params=_cparams("parallel"),
        name="oproj_residual_router",
    )(o, w, x, gate.reshape(B, 1, D), g.reshape(1, D), shift.reshape(B, 1, D),
      scale.reshape(B, 1, D), router_w.T, router_b.reshape(E, 1))


def _route_kernel(lg_ref, tri_ref, idx_ref, gate_ref, rank_ref, cnt_ref, carry_ref):
    i = pl.program_id(0)

    @pl.when(i == 0)
    def _():
        carry_ref[...] = jnp.zeros_like(carry_ref)

    work = lg_ref[...]
    E, tt = work.shape
    eio = lax.broadcasted_iota(I32, (E, tt), 0)
    vals, onehots = [], []
    for k in range(TOP_K):
        m = jnp.max(work, axis=0, keepdims=True)
        ix = jnp.min(jnp.where(work == m, eio, E), axis=0, keepdims=True)
        oh = eio == ix
        work = jnp.where(oh, NEG_INF, work)
        vals.append(m)
        onehots.append(oh)
        idx_ref[k:k + 1, :] = ix
    es = [jnp.exp(v - vals[0]) for v in vals]
    den = es[0] + es[1] + es[2] + es[3]
    for k in range(TOP_K):
        gate_ref[k:k + 1, :] = es[k] / den
    sel = jnp.zeros((E, tt), F32)
    for oh in onehots:
        sel = sel + jnp.where(oh, 1.0, 0.0)
    cnt = _dot(sel.astype(BF16), tri_ref[...]) + carry_ref[:, 0:1]
    for k in range(TOP_K):
        r = jnp.sum(jnp.where(onehots[k], cnt - 1.0, 0.0), axis=0, keepdims=True)
        rank_ref[k:k + 1, :] = r.astype(I32)
    last = jnp.broadcast_to(cnt[:, tt - 1:tt], (E, LANES))
    carry_ref[...] = last
    cnt_ref[...] = last


def _route(logits_t):
    E, T = logits_t.shape
    tt = min(T, 512)
    tri = (jnp.arange(tt)[:, None] <= jnp.arange(tt)[None, :]).astype(BF16)
    tok = pl.BlockSpec((TOP_K, tt), lambda i: (0, i))
    return pl.pallas_call(
        _route_kernel,
        grid=(T // tt,),
        in_specs=[pl.BlockSpec((E, tt), lambda i: (0, i)),
                  pl.BlockSpec((tt, tt), lambda i: (0, 0))],
        out_specs=[tok, tok, tok, pl.BlockSpec((E, LANES), lambda i: (0, 0))],
        out_shape=[jax.ShapeDtypeStruct((TOP_K, T), I32),
                   jax.ShapeDtypeStruct((TOP_K, T), F32),
                   jax.ShapeDtypeStruct((TOP_K, T), I32),
                   jax.ShapeDtypeStruct((E, LANES), F32)],
        scratch_shapes=[pltpu.VMEM((E, LANES), F32)],
        compiler_params=_cparams("arbitrary"),
        name="route_topk_rank",
    )(logits_t, tri)


def _dest_kernel(idx_ref, rank_ref, ps_ref, dest_ref):
    E = ps_ref.shape[0]
    tt = idx_ref.shape[1]
    eio = lax.broadcasted_iota(I32, (E, tt), 0)
    ps = ps_ref[...]
    for k in range(TOP_K):
        oh = eio == idx_ref[k:k + 1, :]
        base = jnp.sum(jnp.where(oh, ps, 0), axis=0, keepdims=True)
        dest_ref[k:k + 1, :] = base + rank_ref[k:k + 1, :]


def _dest(idx_t, rank_t, pstart):
    T = idx_t.shape[1]
    E = pstart.shape[0]
    tt = min(T, 2048)
    tok = pl.BlockSpec((TOP_K, tt), lambda i: (0, i))
    return pl.pallas_call(
        _dest_kernel, grid=(T // tt,),
        in_specs=[tok, tok, pl.BlockSpec((E, 1), lambda i: (0, 0))],
        out_specs=tok,
        out_shape=jax.ShapeDtypeStruct((TOP_K, T), I32),
        compiler_params=_cparams("parallel"), name="route_dest",
    )(idx_t, rank_t, pstart.reshape(E, 1))


ROWS_PER_ISSUE_TRIP = 4


def _row_copy(src, s_row, dst, d_row, sem):
    return pltpu.make_async_copy(src.at[pl.ds(s_row, 1)], dst.at[pl.ds(d_row, 1)], sem)


def _dispatch_kernel(pend_ref, padded_ref, dest_ref, h_ref, xs_hbm, zbuf, sem, zsem, *, tt, tm):
    @pl.when(pl.program_id(0) == 0)
    def _():
        zbuf[...] = jnp.zeros_like(zbuf)

        def zero_copy(e):
            start = pl.multiple_of(pend_ref[e] - tm, tm)
            return pltpu.make_async_copy(zbuf, xs_hbm.at[pl.ds(start, tm)], zsem)

        def start(e, c):
            @pl.when(padded_ref[e] > 0)
            def _():
                zero_copy(e).start()
            return c

        def wait(e, c):
            @pl.when(padded_ref[e] > 0)
            def _():
                zero_copy(e).wait()
            return c

        lax.fori_loop(0, N_EXPERTS, start, 0)
        lax.fori_loop(0, N_EXPERTS, wait, 0)

        def tail_copy(b):
            return pltpu.make_async_copy(zbuf, xs_hbm.at[pl.ds(pl.multiple_of(b * tm, tm), tm)], zsem)

        def tail_start(b, c):
            tail_copy(b).start()
            return c

        def tail_wait(b, c):
            tail_copy(b).wait()
            return c

        first_unused = pend_ref[N_EXPERTS - 1] // tm
        lax.fori_loop(first_unused, xs_hbm.shape[0] // tm, tail_start, 0)
        lax.fori_loop(first_unused, xs_hbm.shape[0] // tm, tail_wait, 0)

    def issue(rr, c):
        for u in range(ROWS_PER_ISSUE_TRIP):
            r = rr * ROWS_PER_ISSUE_TRIP + u
            for k in range(TOP_K):
                _row_copy(h_ref, r, xs_hbm, dest_ref[r * TOP_K + k], sem).start(priority=k % 2)
        return c

    lax.fori_loop(0, tt // ROWS_PER_ISSUE_TRIP, issue, 0)

    def drain(r, c):
        for k in range(TOP_K):
            _row_copy(h_ref, 0, xs_hbm, 0, sem).wait()
        return c

    lax.fori_loop(0, tt, drain, 0)


def _dispatch(pends, padded, dest_flat, h, n_rows, tm):
    T, D = h.shape
    tt = min(T, 256)
    return pl.pallas_call(
        functools.partial(_dispatch_kernel, tt=tt, tm=tm),
        grid_spec=pltpu.PrefetchScalarGridSpec(
            num_scalar_prefetch=2,
            grid=(T // tt,),
            in_specs=[pl.BlockSpec((tt * TOP_K,), lambda i, pe, pa: (i,), memory_space=pltpu.SMEM),
                      pl.BlockSpec((tt, D), lambda i, pe, pa: (i, 0))],
            out_specs=pl.BlockSpec(memory_space=pl.ANY),
            scratch_shapes=[pltpu.VMEM((tm, D), h.dtype), pltpu.SemaphoreType.DMA(()),
                            pltpu.SemaphoreType.DMA(())],
        ),
        out_shape=jax.ShapeDtypeStruct((n_rows, D), h.dtype),
        compiler_params=_cparams("arbitrary"),
        name="moe_dispatch",
    )(pends, padded, dest_flat, h)


def _gmm_kernel(be_ref, nu_ref, xs_ref, wgu_ref, bgu_ref, wd_ref, bd_ref, y_ref):
    del be_ref
    i = pl.program_id(0)
    F = D_FF_EXPERT

    @pl.when(i < nu_ref[0])
    def _():
        x_lo, x_hi = _unpack_bf16_pair(xs_ref[...])
        half = x_lo.shape[1]
        gu = (_dot(x_lo.astype(BF16), wgu_ref[0, 0, :half, :])
              + _dot(x_hi.astype(BF16), wgu_ref[0, 0, half:, :]) + bgu_ref[0, 0])
        g = jnp.minimum(gu[:, :F], SWIGLU_LIMIT)
        u = jnp.clip(gu[:, F:], -SWIGLU_LIMIT, SWIGLU_LIMIT)
        a = g * (1.0 / (1.0 + jnp.exp(-SWIGLU_ALPHA * g))) * (u + 1.0)
        y = _dot(a.astype(BF16), wd_ref[0, 0]) + bd_ref[0, 0]
        y_ref[...] = _pack_bf16_pair(y[:, :half], y[:, half:])

    @pl.when(i >= nu_ref[0])
    def _():
        y_ref[...] = jnp.zeros_like(y_ref)


def _gmm(block_expert, n_used, xs, w_gu, b_gu, w_down, b_down, layer, tm):
    n_rows, Dp = xs.shape
    L, E, D, F2 = w_gu.shape
    F = F2 // 2
    n_blocks = n_rows // tm

    def row_map(i, be, nu):
        return (jnp.minimum(i, nu[0] - 1), 0)

    def w_map(i, be, nu):
        return (layer, be[i], 0, 0)

    return pl.pallas_call(
        _gmm_kernel,
        grid_spec=pltpu.PrefetchScalarGridSpec(
            num_scalar_prefetch=2,
            grid=(n_blocks,),
            in_specs=[pl.BlockSpec((tm, Dp), row_map),
                      pl.BlockSpec((1, 1, D, F2), w_map),
                      pl.BlockSpec((1, 1, 1, F2), w_map),
                      pl.BlockSpec((1, 1, F, D), w_map),
                      pl.BlockSpec((1, 1, 1, D), w_map)],
            out_specs=pl.BlockSpec((tm, Dp), lambda i, be, nu: (i, 0)),
        ),
        out_shape=jax.ShapeDtypeStruct((n_rows, Dp), U32),
        compiler_params=_cparams("arbitrary"),
        name="moe_expert_mlp",
    )(block_expert, n_used, xs, w_gu, b_gu.reshape(L, E, 1, F2), w_down, b_down.reshape(L, E, 1, D))


def _combine_kernel(dest_ref, dest_next_ref, gates_ref, x_ref, gm_ref, gf_ref, y_hbm, o_ref, ybuf, sems,
                    *, tt, final):
    i = pl.program_id(0)
    slot = lax.rem(i, 2)

    def gather(dref, s):
        def issue(rr, c):
            for u in range(ROWS_PER_ISSUE_TRIP):
                r = rr * ROWS_PER_ISSUE_TRIP + u
                for k in range(TOP_K):
                    _row_copy(y_hbm, dref[r * TOP_K + k], ybuf.at[s, k], r, sems.at[s]).start(priority=k % 2)
            return c

        lax.fori_loop(0, tt // ROWS_PER_ISSUE_TRIP, issue, 0)

    @pl.when(i == 0)
    def _():
        gather(dest_ref, 0)

    @pl.when(i + 1 < pl.num_programs(0))
    def _():
        gather(dest_next_ref, 1 - slot)

    def drain(r, c):
        for k in range(TOP_K):
            _row_copy(y_hbm, 0, ybuf.at[slot, k], 0, sems.at[slot]).wait()
        return c

    lax.fori_loop(0, tt, drain, 0)

    g = gates_ref[...]
    half = ybuf.shape[-1]
    moe_lo = moe_hi = None
    for k in range(TOP_K):
        lo, hi = _unpack_bf16_pair(ybuf[slot, k])
        gk = g[:, k:k + 1]
        moe_lo = gk * lo if moe_lo is None else moe_lo + gk * lo
        moe_hi = gk * hi if moe_hi is None else moe_hi + gk * hi
    gm = gm_ref[0]
    x_lo = x_ref[:, :half] + gm[:, :half] * moe_lo
    x_hi = x_ref[:, half:] + gm[:, half:] * moe_hi
    if final:
        ms = (jnp.sum(x_lo * x_lo, axis=-1, keepdims=True)
              + jnp.sum(x_hi * x_hi, axis=-1, keepdims=True)) / (2 * half)
        inv = lax.rsqrt(ms + RMS_EPS)
        x_lo = x_lo * inv * gf_ref[:, :half]
        x_hi = x_hi * inv * gf_ref[:, half:]
    o_ref[:, :half] = x_lo
    o_ref[:, half:] = x_hi


def _combine(dest_flat, gates, x, gate_m, g_final, y, S, final):
    T, D = x.shape
    B = T // S
    tt = min(S, 256)
    per_b = S // tt
    n_tiles = T // tt
    row = pl.BlockSpec((tt, D), lambda i: (i, 0))
    return pl.pallas_call(
        functools.partial(_combine_kernel, tt=tt, final=final),
        grid=(n_tiles,),
        in_specs=[pl.BlockSpec((tt * TOP_K,), lambda i: (i,), memory_space=pltpu.SMEM),
                  pl.BlockSpec((tt * TOP_K,), lambda i: (jnp.minimum(i + 1, n_tiles - 1),),
                               memory_space=pltpu.SMEM),
                  pl.BlockSpec((tt, TOP_K), lambda i: (i, 0)),
                  row,
                  pl.BlockSpec((1, 1, D), lambda i: (i // per_b, 0, 0)),
                  pl.BlockSpec((1, D), lambda i: (0, 0)),
                  pl.BlockSpec(memory_space=pl.ANY)],
        out_specs=row,
        out_shape=jax.ShapeDtypeStruct((T, D), F32),
        scratch_shapes=[pltpu.VMEM((2, TOP_K, tt, D // 2), U32), pltpu.SemaphoreType.DMA((2,))],
        compiler_params=_cparams("arbitrary"),
        name="moe_combine",
    )(dest_flat, dest_flat, gates, x, gate_m.reshape(B, 1, D), g_final.reshape(1, D), y)


def _moe(h, logits_t, x, gate_m, g_final, w_gu, b_gu, w_down, b_down, layer, S, final):
    T, D = h.shape
    E = N_EXPERTS
    tm = 512
    idx_t, gates_t, rank_t, cnt = _route(logits_t)
    counts = cnt[:, 0].astype(I32)
    padded = (counts + tm - 1) // tm * tm
    pends = jnp.cumsum(padded)
    pstart = pends - padded
    n_rows = T * TOP_K + E * tm
    n_blocks = n_rows // tm
    block_start = jnp.arange(n_blocks, dtype=I32) * tm
    block_expert = jnp.minimum(jnp.sum((pends[None, :] <= block_start[:, None]).astype(I32), axis=1), E - 1)
    n_used = (pends[-1:] // tm).astype(I32)
    dest_flat = _dest(idx_t, rank_t, pstart).T.reshape(-1)
    xs = _dispatch(pends, padded, dest_flat, h, n_rows, tm)
    y = _gmm(block_expert, n_used, xs, w_gu, b_gu, w_down, b_down, layer, tm)
    return _combine(dest_flat, gates_t.T, x, gate_m, g_final, y, S, final)


def _dq_kernel(x_ref, g_ref, sh_ref, sc_ref, w_ref, gcq_ref, o_ref):
    h = _norm_mod(x_ref[...], g_ref[...], sh_ref[0], sc_ref[0]).astype(BF16)
    o_ref[...] = _rms(_dot(h, w_ref[...]), gcq_ref[...]).astype(o_ref.dtype)


def _dq(x, g, shift, scale, w, gcq, S):
    T, D = x.shape
    R = w.shape[1]
    B = T // S
    tm = min(S, 512)
    per_b = S // tm
    mod = pl.BlockSpec((1, 1, D), lambda i: (i // per_b, 0, 0))
    return pl.pallas_call(
        _dq_kernel, grid=(T // tm,),
        in_specs=[pl.BlockSpec((tm, D), lambda i: (i, 0)),
                  pl.BlockSpec((1, D), lambda i: (0, 0)), mod, mod,
                  pl.BlockSpec((D, R), lambda i: (0, 0)),
                  pl.BlockSpec((1, R), lambda i: (0, 0))],
        out_specs=pl.BlockSpec((tm, R), lambda i: (i, 0)),
        out_shape=jax.ShapeDtypeStruct((T, R), BF16),
        compiler_params=_cparams("parallel"), name="q_down_norm",
    )(x, g.reshape(1, D), shift.reshape(B, 1, D), scale.reshape(B, 1, D), w, gcq.reshape(1, R))


def _uq_rope_kernel(cq_ref, w_ref, cos_ref, sin_ref, o_ref):
    cq = cq_ref[...]
    cos = cos_ref[...]
    sin = sin_ref[...]
    for h in range(w_ref.shape[0]):
        acc = _dot(cq, w_ref[h])
        q = acc * cos + pltpu.roll(acc, QPAD - QK_ROPE_DIM, 1) * sin
        o_ref[:, h * QPAD:(h + 1) * QPAD] = q.astype(o_ref.dtype)


def _uq_rope(cq, wq, cos_p, sin_p):
    T, R = cq.shape
    H = wq.shape[0]
    tm = min(T, 512)
    tab = pl.BlockSpec((tm, QPAD), lambda i: (i, 0))
    return pl.pallas_call(
        _uq_rope_kernel, grid=(T // tm,),
        in_specs=[pl.BlockSpec((tm, R), lambda i: (i, 0)),
                  pl.BlockSpec((H, R, QPAD), lambda i: (0, 0, 0)), tab, tab],
        out_specs=pl.BlockSpec((tm, H * QPAD), lambda i: (i, 0)),
        out_shape=jax.ShapeDtypeStruct((T, H * QPAD), BF16),
        compiler_params=_cparams("parallel"), name="q_up_rope",
    )(cq, wq, cos_p, sin_p)


def _kv_latent_kernel(x_ref, g_ref, sh_ref, sc_ref, w_ref, gc_ref, cos_ref, sin_ref, ckv_ref, kr_ref):
    h = _norm_mod(x_ref[...], g_ref[...], sh_ref[0], sc_ref[0]).astype(BF16)
    lat = _dot(h, w_ref[...])
    ckv_ref[...] = _rms(lat[:, :KV_LORA_RANK], gc_ref[...]).astype(ckv_ref.dtype)
    t = lat[:, KV_LORA_RANK:]
    kr_ref[...] = (t * cos_ref[...] + pltpu.roll(t, QK_ROPE_DIM, 1) * sin_ref[...]).astype(kr_ref.dtype)


def _kv_latent(x, g, shift, scale, w_ext, g_ckv, cos_k, sin_k, S):
    T, D = x.shape
    B = T // S
    N = w_ext.shape[1]
    tm = min(S, 512)
    per_b = S // tm
    mod = pl.BlockSpec((1, 1, D), lambda i: (i // per_b, 0, 0))
    tab = pl.BlockSpec((tm, LANES), lambda i: (i, 0))
    return pl.pallas_call(
        _kv_latent_kernel, grid=(T // tm,),
        in_specs=[pl.BlockSpec((tm, D), lambda i: (i, 0)),
                  pl.BlockSpec((1, D), lambda i: (0, 0)), mod, mod,
                  pl.BlockSpec((D, N), lambda i: (0, 0)),
                  pl.BlockSpec((1, KV_LORA_RANK), lambda i: (0, 0)), tab, tab],
        out_specs=[pl.BlockSpec((tm, KV_LORA_RANK), lambda i: (i, 0)), tab],
        out_shape=[jax.ShapeDtypeStruct((T, KV_LORA_RANK), BF16),
                   jax.ShapeDtypeStruct((T, LANES), BF16)],
        compiler_params=_cparams("parallel"), name="kv_latent",
    )(x, g.reshape(1, D), shift.reshape(B, 1, D), scale.reshape(B, 1, D), w_ext,
      g_ckv.reshape(1, KV_LORA_RANK), cos_k, sin_k)


def _kv_up_kernel(c_ref, wk_ref, wvt_ref, kr_ref, k_ref, vt_ref, *, tk):
    c = c_ref[...]
    kn = _dot(c, wk_ref[...]).astype(k_ref.dtype)
    kr = kr_ref[...]
    for h in range(MLA_HEADS):
        k_ref[:, h * QPAD:h * QPAD + QK_NOPE_DIM] = kn[:, h * QK_NOPE_DIM:(h + 1) * QK_NOPE_DIM]
        k_ref[:, h * QPAD + QK_NOPE_DIM:(h + 1) * QPAD] = kr
    vt = _dot_nt(wvt_ref[...], c).astype(vt_ref.dtype)
    ones = jnp.ones((VT_ROWS - V_HEAD_DIM, tk), vt_ref.dtype)
    for j in range(vt_ref.shape[0]):
        for h in range(MLA_HEADS):
            vt_ref[j, h * VT_ROWS:h * VT_ROWS + V_HEAD_DIM, :] = vt[h * V_HEAD_DIM:(h + 1) * V_HEAD_DIM,
                                                                    j * tk:(j + 1) * tk]
            vt_ref[j, h * VT_ROWS + V_HEAD_DIM:(h + 1) * VT_ROWS, :] = ones


def _kv_up(ckv, w_uk, w_uv_t, k_rope, tk):
    T, R = ckv.shape
    H = MLA_HEADS
    tm = min(T, 512)
    return pl.pallas_call(
        functools.partial(_kv_up_kernel, tk=tk), grid=(T // tm,),
        in_specs=[pl.BlockSpec((tm, R), lambda i: (i, 0)),
                  pl.BlockSpec((R, H * QK_NOPE_DIM), lambda i: (0, 0)),
                  pl.BlockSpec((H * V_HEAD_DIM, R), lambda i: (0, 0)),
                  pl.BlockSpec((tm, LANES), lambda i: (i, 0))],
        out_specs=[pl.BlockSpec((tm, H * QPAD), lambda i: (i, 0)),
                   pl.BlockSpec((tm // tk, H * VT_ROWS, tk), lambda i: (i, 0, 0))],
        out_shape=[jax.ShapeDtypeStruct((T, H * QPAD), BF16),
                   jax.ShapeDtypeStruct((T // tk, H * VT_ROWS, tk), BF16)],
        compiler_params=_cparams("parallel"), name="kv_up",
    )(ckv, w_uk, w_uv_t, k_rope)


def _mla_attn_kernel(q_ref, k_ref, vt_ref, o_ref, m_ref, acc_ref, s_ref, *, tq, heads):
    qi = pl.program_id(2)
    Dv = V_HEAD_DIM
    m_ref[...] = jnp.full_like(m_ref, NEG_INF)
    acc_ref[...] = jnp.zeros_like(acc_ref)

    def scores(j):
        ks = pl.multiple_of(j * tq, tq)
        return [_dot_nt(k_ref[pl.ds(ks, tq), g * QPAD:(g + 1) * QPAD], q_ref[:, g * QPAD:(g + 1) * QPAD])
                for g in range(heads)]

    def softmax_pv(j, sts, masked):
        for g in range(heads):
            st = sts[g]
            if masked:
                kc = lax.broadcasted_iota(I32, (tq, tq), 0) // CHUNK
                qc = lax.broadcasted_iota(I32, (tq, tq), 1) // CHUNK
                st = jnp.where(kc <= qc, st, NEG_INF)
            m_prev = m_ref[g]
            m_new = jnp.maximum(m_prev, jnp.max(st, axis=0, keepdims=True))
            alpha = jnp.exp2(m_prev - m_new)
            p = jnp.exp2(st - m_new)
            acc_ref[g] = alpha * acc_ref[g] + _dot(vt_ref[j, g * VT_ROWS:(g + 1) * VT_ROWS, :], p.astype(BF16))
            m_ref[g] = m_new

    for g, st in enumerate(scores(0)):
        s_ref[g] = st

    def body(j, c):
        cur = [s_ref[g] for g in range(heads)]
        nxt = scores(j + 1)
        softmax_pv(j, cur, False)
        for g in range(heads):
            s_ref[g] = nxt[g]
        return c

    lax.fori_loop(0, qi, body, 0)
    softmax_pv(qi, [s_ref[g] for g in range(heads)], True)
    for g in range(heads):
        out_t = acc_ref[g, :Dv, :] / acc_ref[g, Dv:Dv + 1, :]
        o_ref[:, g * Dv:(g + 1) * Dv] = out_t.T.astype(o_ref.dtype)


def _mla_attention(q_cat, k_cat, v_t, B, S, tq):
    H = MLA_HEADS
    T = q_cat.shape[0]
    G = 4
    nq = S // tq
    return pl.pallas_call(
        functools.partial(_mla_attn_kernel, tq=tq, heads=G),
        grid=(B, H // G, nq),
        in_specs=[pl.BlockSpec((tq, G * QPAD), lambda b, h, i: (b * nq + i, h)),
                  pl.BlockSpec((S, G * QPAD), lambda b, h, i: (b, h)),
                  pl.BlockSpec((nq, G * VT_ROWS, tq), lambda b, h, i: (b, h, 0))],
        out_specs=pl.BlockSpec((tq, G * V_HEAD_DIM), lambda b, h, i: (b * nq + i, h)),
        out_shape=jax.ShapeDtypeStruct((T, H * V_HEAD_DIM), BF16),
        scratch_shapes=[pltpu.VMEM((G, 1, tq), F32),
                        pltpu.VMEM((G, VT_ROWS, tq), F32), pltpu.VMEM((G, tq, tq), F32)],
        compiler_params=_cparams("parallel", "parallel", "arbitrary"),
        name="mla_attention",
    )(q_cat, k_cat, v_t)


def _rot_cols(w):
    half = QK_ROPE_DIM // 2
    return jnp.concatenate([-w[..., half:], w[..., :half]], axis=-1)


def _q_weights(w_uq):
    R = w_uq.shape[0]
    w = w_uq.reshape(R, MLA_HEADS, QK_NOPE_DIM + QK_ROPE_DIM)
    wn, wr = w[..., :QK_NOPE_DIM], w[..., QK_NOPE_DIM:]
    return jnp.concatenate([wn, wr, _rot_cols(wr)], axis=-1).transpose(1, 0, 2).astype(BF16)


def kernel(x, c, positions, w_mod, b_mod, g_mix, g_ffn, w_qkv_a, w_o_a, g_kv_in, w_mod_kv, b_mod_kv,
           w_dkv, g_ckv, w_uk, w_uv, w_dq_b, g_cq_b, w_uq_b, w_o_b, router_w, router_b, w_gu, b_gu,
           w_down, b_down, g_final):
    B, S, D = x.shape
    T = B * S
    depth = w_mod.shape[0]
    n_a = w_qkv_a.shape[0]
    xt = x.reshape(T, D)

    mod = _adaln(c, w_mod, b_mod)
    mod_kv = _adaln(c, w_mod_kv[None], b_mod_kv[None])[0]

    half = QK_ROPE_DIM // 2
    inv_freq = ROPE_BASE ** (-jnp.arange(half, dtype=F32) / half)
    ang = positions.astype(F32).reshape(T, 1) * inv_freq
    cos_f, sin_f = _cossin(ang.reshape(T * half // LANES, LANES))
    cos = cos_f.reshape(T, half)
    sin = sin_f.reshape(T, half)
    z64 = jnp.zeros((T, QK_ROPE_DIM), F32)
    cos_k = jnp.concatenate([cos, cos, z64], axis=1)
    sin_k = jnp.concatenate([sin, sin, z64], axis=1)
    qs = LOG2_E * (QK_NOPE_DIM + QK_ROPE_DIM) ** -0.5
    cos_q = jnp.concatenate([jnp.ones((T, QK_NOPE_DIM), F32), cos_k], axis=1) * qs
    sin_q = jnp.concatenate([jnp.zeros((T, QK_NOPE_DIM), F32), sin_k], axis=1) * qs

    tq_mla = min(S, 256)
    w_gu_b = w_gu.astype(BF16)
    w_down_b = w_down.astype(BF16)

    shared = None
    for layer in range(depth):
        m = mod[layer]
        shift_a, scale_a, gate_a, shift_m, scale_m, gate_m = [m[:, i * D:(i + 1) * D] for i in range(6)]
        if layer < n_a:
            qkv = _nm_matmul(xt, g_mix[layer], shift_a, scale_a, w_qkv_a[layer].astype(BF16), S,
                             q_cols=SB_HEADS * SB_HEAD_DIM, q_scale=LOG2_E * SB_HEAD_DIM ** -0.5)
            o = _sb_attention(qkv, B, S)
            w_o = w_o_a[layer]
        else:
            i = layer - n_a
            k_cat, v_b = shared
            cq = _dq(xt, g_mix[layer], shift_a, scale_a, w_dq_b[i].astype(BF16), g_cq_b[i], S)
            q_cat = _uq_rope(cq, _q_weights(w_uq_b[i]), cos_q, sin_q)
            o = _mla_attention(q_cat, k_cat, v_b, B, S, tq_mla)
            w_o = w_o_b[i]
        xt, h, logits_t = _oproj(o, w_o.astype(BF16), xt, gate_a, g_ffn[layer], shift_m, scale_m,
                                 router_w[layer], router_b[layer], S)
        xt = _moe(h, logits_t, xt, gate_m, g_final, w_gu_b, b_gu, w_down_b, b_down, layer, S,
                  final=(layer == depth - 1))
        if layer == n_a - 1:
            w_ext = jnp.concatenate([w_dkv, _rot_cols(w_dkv[:, KV_LORA_RANK:])], axis=1).astype(BF16)
            ckv, k_rope = _kv_latent(xt, g_kv_in, mod_kv[:, :D], mod_kv[:, D:], w_ext, g_ckv,
                                     cos_k, sin_k, S)
            shared = _kv_up(ckv, w_uk.astype(BF16), w_uv.T.astype(BF16), k_rope, tq_mla)
    return xt.reshape(B, S, D)
```

```python
import functools

import jax
import jax.numpy as jnp
from jax import lax
from jax.experimental import pallas as pl
from jax.experimental.pallas import tpu as pltpu

F32 = jnp.float32
BF16 = jnp.bfloat16
I32 = jnp.int32
U32 = jnp.uint32

CHUNK = 64
SB_HEADS = 16
SB_HEAD_DIM = 128
MLA_HEADS = 16
QK_NOPE_DIM = 128
QK_ROPE_DIM = 64
V_HEAD_DIM = 128
KV_LORA_RANK = 512
ROPE_BASE = 10000.0
N_EXPERTS = 32
TOP_K = 4
D_FF_EXPERT = 768
SWIGLU_ALPHA = 1.702
SWIGLU_LIMIT = 7.0
RMS_EPS = 1e-6

V7X_VMEM_LIMIT_BYTES = 52 * 1024 * 1024
LANES = 128
QPAD = 256
VT_ROWS = V_HEAD_DIM + 16

NEG_INF = float("-inf")
LOG2_E = 1.4426950408889634
SB_UNDERFLOW_LOG2 = -150.0


def _cparams(*sem):
    return pltpu.CompilerParams(dimension_semantics=sem, vmem_limit_bytes=V7X_VMEM_LIMIT_BYTES)


def _split_bf16(a):
    hi = a.astype(BF16)
    lo = (a - hi.astype(F32)).astype(BF16)
    return hi, lo


def _dot(a, b):
    return jnp.dot(a, b, preferred_element_type=F32)


def _dot_nt(a, b):
    return lax.dot_general(a, b, (((1,), (1,)), ((), ())), preferred_element_type=F32)


def _pack_bf16_pair(lo, hi):
    lo_bits = lax.bitcast_convert_type(lo.astype(BF16).astype(F32), U32) >> 16
    hi_bits = lax.bitcast_convert_type(hi.astype(BF16).astype(F32), U32) & jnp.uint32(0xFFFF0000)
    return hi_bits | lo_bits


def _unpack_bf16_pair(w):
    lo = lax.bitcast_convert_type(w << 16, F32)
    hi = lax.bitcast_convert_type(w & jnp.uint32(0xFFFF0000), F32)
    return lo, hi


def _rms(x, g):
    ms = jnp.mean(x * x, axis=-1, keepdims=True)
    return x * lax.rsqrt(ms + RMS_EPS) * g


def _norm_mod(x, g, shift, scale):
    return _rms(x, g) * (1.0 + scale) + shift


def _adaln_kernel(c_ref, w_ref, b_ref, o_ref):
    c = c_ref[...]
    s = c * (1.0 / (1.0 + jnp.exp(-c)))
    s_hi, s_lo = _split_bf16(s)
    w_hi, w_lo = _split_bf16(w_ref[0])
    o_ref[0] = _dot(s_hi, w_hi) + _dot(s_lo, w_hi) + _dot(s_hi, w_lo) + b_ref[0]


def _adaln(c, w, b):
    L, D, N = w.shape
    B = c.shape[0]
    tn = 1024
    return pl.pallas_call(
        _adaln_kernel,
        grid=(L, N // tn),
        in_specs=[pl.BlockSpec((B, D), lambda l, j: (0, 0)),
                  pl.BlockSpec((1, D, tn), lambda l, j: (l, 0, j)),
                  pl.BlockSpec((1, 1, tn), lambda l, j: (l, 0, j))],
        out_specs=pl.BlockSpec((1, B, tn), lambda l, j: (l, 0, j)),
        out_shape=jax.ShapeDtypeStruct((L, B, N), F32),
        compiler_params=_cparams("parallel", "parallel"),
        name="adaln",
    )(c, w, b.reshape(L, 1, N))


def _cossin_kernel(a_ref, c_ref, s_ref):
    a = a_ref[...]
    c_ref[...] = jnp.cos(a)
    s_ref[...] = jnp.sin(a)


def _cossin(ang):
    R = ang.shape[0]
    tr = min(R, 512)
    spec = pl.BlockSpec((tr, LANES), lambda i: (i, 0))
    return pl.pallas_call(
        _cossin_kernel, grid=(R // tr,), in_specs=[spec], out_specs=[spec, spec],
        out_shape=[jax.ShapeDtypeStruct(ang.shape, F32)] * 2,
        compiler_params=_cparams("parallel"), name="cossin",
    )(ang)


def _nm_matmul_kernel(x_ref, g_ref, sh_ref, sc_ref, w_ref, o_ref, h_ref, *, n_scaled, col_scale):
    j = pl.program_id(1)

    @pl.when(j == 0)
    def _():
        h_ref[...] = _norm_mod(x_ref[...], g_ref[...], sh_ref[0], sc_ref[0]).astype(BF16)

    acc = _dot(h_ref[...], w_ref[...])
    s = jnp.where(j < n_scaled, col_scale, 1.0).astype(F32)
    o_ref[...] = (acc * s).astype(o_ref.dtype)


def _nm_matmul(x, g, shift, scale, w, S, *, q_cols=0, q_scale=1.0):
    T, D = x.shape
    N = w.shape[1]
    tm = min(S, 512)
    tn = 1024
    per_b = S // tm
    B = T // S
    mod_spec = pl.BlockSpec((1, 1, D), lambda i, j: (i // per_b, 0, 0))
    return pl.pallas_call(
        functools.partial(_nm_matmul_kernel, n_scaled=q_cols // tn, col_scale=q_scale),
        grid=(T // tm, N // tn),
        in_specs=[pl.BlockSpec((tm, D), lambda i, j: (i, 0)),
                  pl.BlockSpec((1, D), lambda i, j: (0, 0)),
                  mod_spec, mod_spec,
                  pl.BlockSpec((D, tn), lambda i, j: (0, j))],
        out_specs=pl.BlockSpec((tm, tn), lambda i, j: (i, j)),
        out_shape=jax.ShapeDtypeStruct((T, N), BF16),
        scratch_shapes=[pltpu.VMEM((tm, D), BF16)],
        compiler_params=_cparams("parallel", "arbitrary"),
        name="norm_mod_matmul",
    )(x, g.reshape(1, D), shift.reshape(B, 1, D), scale.reshape(B, 1, D), w)


def _sb_attn_kernel(q_ref, k_ref, v_ref, tri_ref, o_ref, acc_ref, run_ref, *, tq, tk, heads):
    qi = pl.program_id(2)
    Dh = SB_HEAD_DIM
    tri = tri_ref[...]
    n_kb = ((qi + 1) * tq) // tk
    n_diag = tq // tk
    t_loc = lax.broadcasted_iota(I32, (tq, tk), 0)
    s_loc = lax.broadcasted_iota(I32, (tq, tk), 1)
    acc_ref[...] = jnp.zeros_like(acc_ref)
    run_ref[...] = jnp.zeros_like(run_ref)

    def block(j, mask):
        ks = pl.multiple_of(j * tk, tk)
        heads_cols = [slice(g * Dh, (g + 1) * Dh) for g in range(heads)]
        zs = [_dot_nt(q_ref[:, c], k_ref[pl.ds(ks, tk), c]) for c in heads_cols]
        ps = []
        for z in zs:
            lk = jnp.minimum(-z, 0.0) - jnp.log2(1.0 + jnp.exp2(-jnp.abs(z)))
            if mask is not None:
                lk = jnp.where(mask, lk, 0.0)
            ps.append(_dot(lk.astype(BF16), tri))
        worst = None
        for g in range(heads):
            run = run_ref[g]
            a = jnp.exp2(zs[g] + ps[g][:, :tk] + run)
            if mask is not None:
                a = jnp.where(mask, a, 0.0)
            acc_ref[g] += _dot(a.astype(BF16), v_ref[pl.ds(ks, tk), heads_cols[g]])
            run = run + ps[g][:, tk:]
            run_ref[g] = run
            m = jnp.max(run)
            worst = m if worst is None else jnp.maximum(worst, m)
        return worst

    for d in range(n_diag):
        j = n_kb - 1 - d
        worst = block(j, (j * tk - qi * tq + s_loc) < t_loc)

    def cond(c):
        j, worst = c
        return jnp.logical_and(j >= 0, worst > SB_UNDERFLOW_LOG2)

    def body(c):
        j, _ = c
        return j - 1, block(j, None)

    lax.while_loop(cond, body, (n_kb - n_diag - 1, worst))
    for g in range(heads):
        o_ref[:, g * Dh:(g + 1) * Dh] = acc_ref[g].astype(o_ref.dtype)


def _sb_attention(qkv, B, S):
    H, Dh = SB_HEADS, SB_HEAD_DIM
    T = qkv.shape[0]
    tk = 128
    tq = min(S, 128)
    G = 8
    nq = S // tq
    hb = H // G
    jj = jnp.arange(tk)[:, None]
    ss = jnp.arange(tk + LANES)[None, :]
    tri = ((jj >= ss) | (ss >= tk)).astype(BF16)
    return pl.pallas_call(
        functools.partial(_sb_attn_kernel, tq=tq, tk=tk, heads=G),
        grid=(B, hb, nq),
        in_specs=[pl.BlockSpec((tq, G * Dh), lambda b, h, i: (b * nq + i, h)),
                  pl.BlockSpec((S, G * Dh), lambda b, h, i: (b, hb + h)),
                  pl.BlockSpec((S, G * Dh), lambda b, h, i: (b, 2 * hb + h)),
                  pl.BlockSpec((tk, tk + LANES), lambda b, h, i: (0, 0))],
        out_specs=pl.BlockSpec((tq, G * Dh), lambda b, h, i: (b * nq + i, h)),
        out_shape=jax.ShapeDtypeStruct((T, H * Dh), BF16),
        scratch_shapes=[pltpu.VMEM((G, tq, Dh), F32), pltpu.VMEM((G, tq, LANES), F32)],
        compiler_params=_cparams("parallel", "parallel", "arbitrary"),
        name="sb_attention",
    )(qkv, qkv, qkv, tri)


def _oproj_kernel(o_ref, w_ref, x_ref, ga_ref, g_ref, sh_ref, sc_ref, rwt_ref, rb_ref,
                  xo_ref, h_ref, lg_ref):
    mix = _dot(o_ref[...], w_ref[...])
    xn = x_ref[...] + ga_ref[0] * mix
    xo_ref[...] = xn
    h = _norm_mod(xn, g_ref[...], sh_ref[0], sc_ref[0])
    half = h.shape[1] // 2
    h_ref[...] = _pack_bf16_pair(h[:, :half], h[:, half:])
    h_hi, h_lo = _split_bf16(h)
    r_hi, r_lo = _split_bf16(rwt_ref[...])
    lg_ref[...] = _dot_nt(r_hi, h_hi) + _dot_nt(r_lo, h_hi) + _dot_nt(r_hi, h_lo) + rb_ref[...]


def _oproj(o, w, x, gate, g, shift, scale, router_w, router_b, S):
    T, D = x.shape
    E = router_w.shape[1]
    B = T // S
    tm = min(S, 512)
    per_b = S // tm
    row = pl.BlockSpec((tm, D), lambda i: (i, 0))
    mod = pl.BlockSpec((1, 1, D), lambda i: (i // per_b, 0, 0))
    return pl.pallas_call(
        _oproj_kernel,
        grid=(T // tm,),
        in_specs=[row,
                  pl.BlockSpec((D, D), lambda i: (0, 0), pipeline_mode=pl.Buffered(1)),
                  row, mod,
                  pl.BlockSpec((1, D), lambda i: (0, 0)),
                  mod, mod,
                  pl.BlockSpec((E, D), lambda i: (0, 0)),
                  pl.BlockSpec((E, 1), lambda i: (0, 0))],
        out_specs=[row, pl.BlockSpec((tm, D // 2), lambda i: (i, 0)), pl.BlockSpec((E, tm), lambda i: (0, i))],
        out_shape=[jax.ShapeDtypeStruct((T, D), F32),
                   jax.ShapeDtypeStruct((T, D // 2), U32),
                   jax.ShapeDtypeStruct((E, T), F32)],
        compiler_params=_cparams("parallel"),
        name="oproj_residual_router",
    )(o, w, x, gate.reshape(B, 1, D), g.reshape(1, D), shift.reshape(B, 1, D),
      scale.reshape(B, 1, D), router_w.T, router_b.reshape(E, 1))


def _route_kernel(lg_ref, tri_ref, idx_ref, gate_ref, rank_ref, cnt_ref, carry_ref):
    i = pl.program_id(0)

    @pl.when(i == 0)
    def _():
        carry_ref[...] = jnp.zeros_like(carry_ref)

    work = lg_ref[...]
    E, tt = work.shape
    eio = lax.broadcasted_iota(I32, (E, tt), 0)
    vals, onehots = [], []
    for k in range(TOP_K):
        m = jnp.max(work, axis=0, keepdims=True)
        ix = jnp.min(jnp.where(work == m, eio, E), axis=0, keepdims=True)
        oh = eio == ix
        work = jnp.where(oh, NEG_INF, work)
        vals.append(m)
        onehots.append(oh)
        idx_ref[k:k + 1, :] = ix
    es = [jnp.exp(v - vals[0]) for v in vals]
    den = es[0] + es[1] + es[2] + es[3]
    for k in range(TOP_K):
        gate_ref[k:k + 1, :] = es[k] / den
    sel = jnp.zeros((E, tt), F32)
    for oh in onehots:
        sel = sel + jnp.where(oh, 1.0, 0.0)
    cnt = _dot(sel.astype(BF16), tri_ref[...]) + carry_ref[:, 0:1]
    for k in range(TOP_K):
        r = jnp.sum(jnp.where(onehots[k], cnt - 1.0, 0.0), axis=0, keepdims=True)
        rank_ref[k:k + 1, :] = r.astype(I32)
    last = jnp.broadcast_to(cnt[:, tt - 1:tt], (E, LANES))
    carry_ref[...] = last
    cnt_ref[...] = last


def _route(logits_t):
    E, T = logits_t.shape
    tt = min(T, 512)
    tri = (jnp.arange(tt)[:, None] <= jnp.arange(tt)[None, :]).astype(BF16)
    tok = pl.BlockSpec((TOP_K, tt), lambda i: (0, i))
    return pl.pallas_call(
        _route_kernel,
        grid=(T // tt,),
        in_specs=[pl.BlockSpec((E, tt), lambda i: (0, i)),
                  pl.BlockSpec((tt, tt), lambda i: (0, 0))],
        out_specs=[tok, tok, tok, pl.BlockSpec((E, LANES), lambda i: (0, 0))],
        out_shape=[jax.ShapeDtypeStruct((TOP_K, T), I32),
                   jax.ShapeDtypeStruct((TOP_K, T), F32),
                   jax.ShapeDtypeStruct((TOP_K, T), I32),
                   jax.ShapeDtypeStruct((E, LANES), F32)],
        scratch_shapes=[pltpu.VMEM((E, LANES), F32)],
        compiler_params=_cparams("arbitrary"),
        name="route_topk_rank",
    )(logits_t, tri)


def _dest_kernel(idx_ref, rank_ref, ps_ref, dest_ref):
    E = ps_ref.shape[0]
    tt = idx_ref.shape[1]
    eio = lax.broadcasted_iota(I32, (E, tt), 0)
    ps = ps_ref[...]
    for k in range(TOP_K):
        oh = eio == idx_ref[k:k + 1, :]
        base = jnp.sum(jnp.where(oh, ps, 0), axis=0, keepdims=True)
        dest_ref[k:k + 1, :] = base + rank_ref[k:k + 1, :]


def _dest(idx_t, rank_t, pstart):
    T = idx_t.shape[1]
    E = pstart.shape[0]
    tt = min(T, 2048)
    tok = pl.BlockSpec((TOP_K, tt), lambda i: (0, i))
    return pl.pallas_call(
        _dest_kernel, grid=(T // tt,),
        in_specs=[tok, tok, pl.BlockSpec((E, 1), lambda i: (0, 0))],
        out_specs=tok,
        out_shape=jax.ShapeDtypeStruct((TOP_K, T), I32),
        compiler_params=_cparams("parallel"), name="route_dest",
    )(idx_t, rank_t, pstart.reshape(E, 1))


ROWS_PER_ISSUE_TRIP = 8


def _row_copy(src, s_row, dst, d_row, sem):
    return pltpu.make_async_copy(src.at[pl.ds(s_row, 1)], dst.at[pl.ds(d_row, 1)], sem)


def _dispatch_kernel(pend_ref, padded_ref, dest_ref, h_ref, xs_hbm, zbuf, sem, zsem, *, tt, tm):
    @pl.when(pl.program_id(0) == 0)
    def _():
        zbuf[...] = jnp.zeros_like(zbuf)

        def zero_copy(e):
            start = pl.multiple_of(pend_ref[e] - tm, tm)
            return pltpu.make_async_copy(zbuf, xs_hbm.at[pl.ds(start, tm)], zsem)

        def start(e, c):
            @pl.when(padded_ref[e] > 0)
            def _():
                zero_copy(e).start()
            return c

        def wait(e, c):
            @pl.when(padded_ref[e] > 0)
            def _():
                zero_copy(e).wait()
            return c

        lax.fori_loop(0, N_EXPERTS, start, 0)
        lax.fori_loop(0, N_EXPERTS, wait, 0)

        def tail_copy(b):
            return pltpu.make_async_copy(zbuf, xs_hbm.at[pl.ds(pl.multiple_of(b * tm, tm), tm)], zsem)

        def tail_start(b, c):
            tail_copy(b).start()
            return c

        def tail_wait(b, c):
            tail_copy(b).wait()
            return c

        first_unused = pend_ref[N_EXPERTS - 1] // tm
        lax.fori_loop(first_unused, xs_hbm.shape[0] // tm, tail_start, 0)
        lax.fori_loop(first_unused, xs_hbm.shape[0] // tm, tail_wait, 0)

    def issue(rr, c):
        for u in range(ROWS_PER_ISSUE_TRIP):
            r = pl.multiple_of(rr * ROWS_PER_ISSUE_TRIP, ROWS_PER_ISSUE_TRIP) + u
            for k in range(TOP_K):
                _row_copy(h_ref, r, xs_hbm, dest_ref[r * TOP_K + k], sem).start(priority=k % 2)
        return c

    lax.fori_loop(0, tt // ROWS_PER_ISSUE_TRIP, issue, 0)

    def drain(r, c):
        for k in range(TOP_K):
            _row_copy(h_ref, 0, xs_hbm, 0, sem).wait()
        return c

    lax.fori_loop(0, tt, drain, 0)


def _dispatch(pends, padded, dest_flat, h, n_rows, tm):
    T, D = h.shape
    tt = min(T, 256)
    return pl.pallas_call(
        functools.partial(_dispatch_kernel, tt=tt, tm=tm),
        grid_spec=pltpu.PrefetchScalarGridSpec(
            num_scalar_prefetch=2,
            grid=(T // tt,),
            in_specs=[pl.BlockSpec((tt * TOP_K,), lambda i, pe, pa: (i,), memory_space=pltpu.SMEM),
                      pl.BlockSpec((tt, D), lambda i, pe, pa: (i, 0))],
            out_specs=pl.BlockSpec(memory_space=pl.ANY),
            scratch_shapes=[pltpu.VMEM((tm, D), h.dtype), pltpu.SemaphoreType.DMA(()),
                            pltpu.SemaphoreType.DMA(())],
        ),
        out_shape=jax.ShapeDtypeStruct((n_rows, D), h.dtype),
        compiler_params=_cparams("arbitrary"),
        name="moe_dispatch",
    )(pends, padded, dest_flat, h)


def _gmm_kernel(be_ref, nu_ref, xs_ref, wgu_ref, bgu_ref, wd_ref, bd_ref, y_ref):
    del be_ref
    i = pl.program_id(0)
    F = D_FF_EXPERT

    @pl.when(i < nu_ref[0])
    def _():
        x_lo, x_hi = _unpack_bf16_pair(xs_ref[...])
        half = x_lo.shape[1]
        gu = (_dot(x_lo.astype(BF16), wgu_ref[0, 0, :half, :])
              + _dot(x_hi.astype(BF16), wgu_ref[0, 0, half:, :]) + bgu_ref[0, 0])
        g = jnp.minimum(gu[:, :F], SWIGLU_LIMIT)
        u = jnp.clip(gu[:, F:], -SWIGLU_LIMIT, SWIGLU_LIMIT)
        a = g * (1.0 / (1.0 + jnp.exp(-SWIGLU_ALPHA * g))) * (u + 1.0)
        y = _dot(a.astype(BF16), wd_ref[0, 0]) + bd_ref[0, 0]
        y_ref[...] = _pack_bf16_pair(y[:, :half], y[:, half:])

    @pl.when(i >= nu_ref[0])
    def _():
        y_ref[...] = jnp.zeros_like(y_ref)


def _gmm(block_expert, n_used, xs, w_gu, b_gu, w_down, b_down, layer, tm):
    n_rows, Dp = xs.shape
    L, E, D, F2 = w_gu.shape
    F = F2 // 2
    n_blocks = n_rows // tm

    def row_map(i, be, nu):
        return (jnp.minimum(i, nu[0] - 1), 0)

    def w_map(i, be, nu):
        return (layer, be[i], 0, 0)

    return pl.pallas_call(
        _gmm_kernel,
        grid_spec=pltpu.PrefetchScalarGridSpec(
            num_scalar_prefetch=2,
            grid=(n_blocks,),
            in_specs=[pl.BlockSpec((tm, Dp), row_map),
                      pl.BlockSpec((1, 1, D, F2), w_map),
                      pl.BlockSpec((1, 1, 1, F2), w_map),
                      pl.BlockSpec((1, 1, F, D), w_map),
                      pl.BlockSpec((1, 1, 1, D), w_map)],
            out_specs=pl.BlockSpec((tm, Dp), lambda i, be, nu: (i, 0)),
        ),
        out_shape=jax.ShapeDtypeStruct((n_rows, Dp), U32),
        compiler_params=_cparams("arbitrary"),
        name="moe_expert_mlp",
    )(block_expert, n_used, xs, w_gu, b_gu.reshape(L, E, 1, F2), w_down, b_down.reshape(L, E, 1, D))


def _combine_kernel(dest_ref, dest_next_ref, gates_ref, x_ref, gm_ref, gf_ref, y_hbm, o_ref, ybuf, sems,
                    *, tt, final):
    i = pl.program_id(0)
    slot = lax.rem(i, 2)

    def gather(dref, s):
        def issue(rr, c):
            for u in range(ROWS_PER_ISSUE_TRIP):
                r = pl.multiple_of(rr * ROWS_PER_ISSUE_TRIP, ROWS_PER_ISSUE_TRIP) + u
                for k in range(TOP_K):
                    _row_copy(y_hbm, dref[r * TOP_K + k], ybuf.at[s, k], r, sems.at[s]).start(priority=k % 2)
            return c

        lax.fori_loop(0, tt // ROWS_PER_ISSUE_TRIP, issue, 0)

    @pl.when(i == 0)
    def _():
        gather(dest_ref, 0)

    @pl.when(i + 1 < pl.num_programs(0))
    def _():
        gather(dest_next_ref, 1 - slot)

    def drain(r, c):
        for k in range(TOP_K):
            _row_copy(y_hbm, 0, ybuf.at[slot, k], 0, sems.at[slot]).wait()
        return c

    lax.fori_loop(0, tt, drain, 0)

    g = gates_ref[...]
    half = ybuf.shape[-1]
    moe_lo = moe_hi = None
    for k in range(TOP_K):
        lo, hi = _unpack_bf16_pair(ybuf[slot, k])
        gk = g[:, k:k + 1]
        moe_lo = gk * lo if moe_lo is None else moe_lo + gk * lo
        moe_hi = gk * hi if moe_hi is None else moe_hi + gk * hi
    gm = gm_ref[0]
    x_lo = x_ref[:, :half] + gm[:, :half] * moe_lo
    x_hi = x_ref[:, half:] + gm[:, half:] * moe_hi
    if final:
        ms = (jnp.sum(x_lo * x_lo, axis=-1, keepdims=True)
              + jnp.sum(x_hi * x_hi, axis=-1, keepdims=True)) / (2 * half)
        inv = lax.rsqrt(ms + RMS_EPS)
        x_lo = x_lo * inv * gf_ref[:, :half]
        x_hi = x_hi * inv * gf_ref[:, half:]
    o_ref[:, :half] = x_lo
    o_ref[:, half:] = x_hi


def _combine(dest_flat, gates, x, gate_m, g_final, y, S, final):
    T, D = x.shape
    B = T // S
    tt = min(S, 256)
    per_b = S // tt
    n_tiles = T // tt
    row = pl.BlockSpec((tt, D), lambda i: (i, 0))
    return pl.pallas_call(
        functools.partial(_combine_kernel, tt=tt, final=final),
        grid=(n_tiles,),
        in_specs=[pl.BlockSpec((tt * TOP_K,), lambda i: (i,), memory_space=pltpu.SMEM),
                  pl.BlockSpec((tt * TOP_K,), lambda i: (jnp.minimum(i + 1, n_tiles - 1),),
                               memory_space=pltpu.SMEM),
                  pl.BlockSpec((tt, TOP_K), lambda i: (i, 0)),
                  row,
                  pl.BlockSpec((1, 1, D), lambda i: (i // per_b, 0, 0)),
                  pl.BlockSpec((1, D), lambda i: (0, 0)),
                  pl.BlockSpec(memory_space=pl.ANY)],
        out_specs=row,
        out_shape=jax.ShapeDtypeStruct((T, D), F32),
        scratch_shapes=[pltpu.VMEM((2, TOP_K, tt, D // 2), U32), pltpu.SemaphoreType.DMA((2,))],
        compiler_params=_cparams("arbitrary"),
        name="moe_combine",
    )(dest_flat, dest_flat, gates, x, gate_m.reshape(B, 1, D), g_final.reshape(1, D), y)


def _moe(h, logits_t, x, gate_m, g_final, w_gu, b_gu, w_down, b_down, layer, S, final):
    T, D = h.shape
    E = N_EXPERTS
    tm = 512
    idx_t, gates_t, rank_t, cnt = _route(logits_t)
    counts = cnt[:, 0].astype(I32)
    padded = (counts + tm - 1) // tm * tm
    pends = jnp.cumsum(padded)
    pstart = pends - padded
    n_rows = T * TOP_K + E * tm
    n_blocks = n_rows // tm
    block_start = jnp.arange(n_blocks, dtype=I32) * tm
    block_expert = jnp.minimum(jnp.sum((pends[None, :] <= block_start[:, None]).astype(I32), axis=1), E - 1)
    n_used = (pends[-1:] // tm).astype(I32)
    dest_flat = _dest(idx_t, rank_t, pstart).T.reshape(-1)
    xs = _dispatch(pends, padded, dest_flat, h, n_rows, tm)
    y = _gmm(block_expert, n_used, xs, w_gu, b_gu, w_down, b_down, layer, tm)
    return _combine(dest_flat, gates_t.T, x, gate_m, g_final, y, S, final)


def _dq_kernel(x_ref, g_ref, sh_ref, sc_ref, w_ref, gcq_ref, o_ref):
    h = _norm_mod(x_ref[...], g_ref[...], sh_ref[0], sc_ref[0]).astype(BF16)
    o_ref[...] = _rms(_dot(h, w_ref[...]), gcq_ref[...]).astype(o_ref.dtype)


def _dq(x, g, shift, scale, w, gcq, S):
    T, D = x.shape
    R = w.shape[1]
    B = T // S
    tm = min(S, 512)
    per_b = S // tm
    mod = pl.BlockSpec((1, 1, D), lambda i: (i // per_b, 0, 0))
    return pl.pallas_call(
        _dq_kernel, grid=(T // tm,),
        in_specs=[pl.BlockSpec((tm, D), lambda i: (i, 0)),
                  pl.BlockSpec((1, D), lambda i: (0, 0)), mod, mod,
                  pl.BlockSpec((D, R), lambda i: (0, 0)),
                  pl.BlockSpec((1, R), lambda i: (0, 0))],
        out_specs=pl.BlockSpec((tm, R), lambda i: (i, 0)),
        out_shape=jax.ShapeDtypeStruct((T, R), BF16),
        compiler_params=_cparams("parallel"), name="q_down_norm",
    )(x, g.reshape(1, D), shift.reshape(B, 1, D), scale.reshape(B, 1, D), w, gcq.reshape(1, R))


def _uq_rope_kernel(cq_ref, w_ref, cos_ref, sin_ref, o_ref):
    cq = cq_ref[...]
    cos = cos_ref[...]
    sin = sin_ref[...]
    for h in range(w_ref.shape[0]):
        acc = _dot(cq, w_ref[h])
        q = acc * cos + pltpu.roll(acc, QPAD - QK_ROPE_DIM, 1) * sin
        o_ref[:, h * QPAD:(h + 1) * QPAD] = q.astype(o_ref.dtype)


def _uq_rope(cq, wq, cos_p, sin_p):
    T, R = cq.shape
    H = wq.shape[0]
    tm = min(T, 512)
    tab = pl.BlockSpec((tm, QPAD), lambda i: (i, 0))
    return pl.pallas_call(
        _uq_rope_kernel, grid=(T // tm,),
        in_specs=[pl.BlockSpec((tm, R), lambda i: (i, 0)),
                  pl.BlockSpec((H, R, QPAD), lambda i: (0, 0, 0)), tab, tab],
        out_specs=pl.BlockSpec((tm, H * QPAD), lambda i: (i, 0)),
        out_shape=jax.ShapeDtypeStruct((T, H * QPAD), BF16),
        compiler_params=_cparams("parallel"), name="q_up_rope",
    )(cq, wq, cos_p, sin_p)


def _kv_latent_kernel(x_ref, g_ref, sh_ref, sc_ref, w_ref, gc_ref, cos_ref, sin_ref, ckv_ref, kr_ref):
    h = _norm_mod(x_ref[...], g_ref[...], sh_ref[0], sc_ref[0]).astype(BF16)
    lat = _dot(h, w_ref[...])
    ckv_ref[...] = _rms(lat[:, :KV_LORA_RANK], gc_ref[...]).astype(ckv_ref.dtype)
    t = lat[:, KV_LORA_RANK:]
    kr_ref[...] = (t * cos_ref[...] + pltpu.roll(t, QK_ROPE_DIM, 1) * sin_ref[...]).astype(kr_ref.dtype)


def _kv_latent(x, g, shift, scale, w_ext, g_ckv, cos_k, sin_k, S):
    T, D = x.shape
    B = T // S
    N = w_ext.shape[1]
    tm = min(S, 512)
    per_b = S // tm
    mod = pl.BlockSpec((1, 1, D), lambda i: (i // per_b, 0, 0))
    tab = pl.BlockSpec((tm, LANES), lambda i: (i, 0))
    return pl.pallas_call(
        _kv_latent_kernel, grid=(T // tm,),
        in_specs=[pl.BlockSpec((tm, D), lambda i: (i, 0)),
                  pl.BlockSpec((1, D), lambda i: (0, 0)), mod, mod,
                  pl.BlockSpec((D, N), lambda i: (0, 0)),
                  pl.BlockSpec((1, KV_LORA_RANK), lambda i: (0, 0)), tab, tab],
        out_specs=[pl.BlockSpec((tm, KV_LORA_RANK), lambda i: (i, 0)), tab],
        out_shape=[jax.ShapeDtypeStruct((T, KV_LORA_RANK), BF16),
                   jax.ShapeDtypeStruct((T, LANES), BF16)],
        compiler_params=_cparams("parallel"), name="kv_latent",
    )(x, g.reshape(1, D), shift.reshape(B, 1, D), scale.reshape(B, 1, D), w_ext,
      g_ckv.reshape(1, KV_LORA_RANK), cos_k, sin_k)


def _kv_up_kernel(c_ref, wk_ref, wvt_ref, kr_ref, k_ref, vt_ref, *, tk):
    c = c_ref[...]
    kn = _dot(c, wk_ref[...]).astype(k_ref.dtype)
    kr = kr_ref[...]
    for h in range(MLA_HEADS):
        k_ref[:, h * QPAD:h * QPAD + QK_NOPE_DIM] = kn[:, h * QK_NOPE_DIM:(h + 1) * QK_NOPE_DIM]
        k_ref[:, h * QPAD + QK_NOPE_DIM:(h + 1) * QPAD] = kr
    vt = _dot_nt(wvt_ref[...], c).astype(vt_ref.dtype)
    ones = jnp.ones((VT_ROWS - V_HEAD_DIM, tk), vt_ref.dtype)
    for j in range(vt_ref.shape[0]):
        for h in range(MLA_HEADS):
            vt_ref[j, h * VT_ROWS:h * VT_ROWS + V_HEAD_DIM, :] = vt[h * V_HEAD_DIM:(h + 1) * V_HEAD_DIM,
                                                                    j * tk:(j + 1) * tk]
            vt_ref[j, h * VT_ROWS + V_HEAD_DIM:(h + 1) * VT_ROWS, :] = ones


def _kv_up(ckv, w_uk, w_uv_t, k_rope, tk):
    T, R = ckv.shape
    H = MLA_HEADS
    tm = min(T, 512)
    return pl.pallas_call(
        functools.partial(_kv_up_kernel, tk=tk), grid=(T // tm,),
        in_specs=[pl.BlockSpec((tm, R), lambda i: (i, 0)),
                  pl.BlockSpec((R, H * QK_NOPE_DIM), lambda i: (0, 0)),
                  pl.BlockSpec((H * V_HEAD_DIM, R), lambda i: (0, 0)),
                  pl.BlockSpec((tm, LANES), lambda i: (i, 0))],
        out_specs=[pl.BlockSpec((tm, H * QPAD), lambda i: (i, 0)),
                   pl.BlockSpec((tm // tk, H * VT_ROWS, tk), lambda i: (i, 0, 0))],
        out_shape=[jax.ShapeDtypeStruct((T, H * QPAD), BF16),
                   jax.ShapeDtypeStruct((T // tk, H * VT_ROWS, tk), BF16)],
        compiler_params=_cparams("parallel"), name="kv_up",
    )(ckv, w_uk, w_uv_t, k_rope)


def _mla_attn_kernel(q_ref, k_ref, vt_ref, o_ref, m_ref, acc_ref, s_ref, *, tq, heads):
    qi = pl.program_id(2)
    Dv = V_HEAD_DIM
    m_ref[...] = jnp.full_like(m_ref, NEG_INF)
    acc_ref[...] = jnp.zeros_like(acc_ref)

    def scores(j):
        ks = pl.multiple_of(j * tq, tq)
        return [_dot_nt(k_ref[pl.ds(ks, tq), g * QPAD:(g + 1) * QPAD], q_ref[:, g * QPAD:(g + 1) * QPAD])
                for g in range(heads)]

    def softmax_pv(j, sts, masked):
        for g in range(heads):
            st = sts[g]
            if masked:
                kc = lax.broadcasted_iota(I32, (tq, tq), 0) // CHUNK
                qc = lax.broadcasted_iota(I32, (tq, tq), 1) // CHUNK
                st = jnp.where(kc <= qc, st, NEG_INF)
            m_prev = m_ref[g]
            m_new = jnp.maximum(m_prev, jnp.max(st, axis=0, keepdims=True))
            alpha = jnp.exp2(m_prev - m_new)
            p = jnp.exp2(st - m_new)
            acc_ref[g] = alpha * acc_ref[g] + _dot(vt_ref[j, g * VT_ROWS:(g + 1) * VT_ROWS, :], p.astype(BF16))
            m_ref[g] = m_new

    for g, st in enumerate(scores(0)):
        s_ref[g] = st

    def body(j, c):
        cur = [s_ref[g] for g in range(heads)]
        nxt = scores(j + 1)
        softmax_pv(j, cur, False)
        for g in range(heads):
            s_ref[g] = nxt[g]
        return c

    lax.fori_loop(0, qi, body, 0)
    softmax_pv(qi, [s_ref[g] for g in range(heads)], True)
    for g in range(heads):
        out_t = acc_ref[g, :Dv, :] / acc_ref[g, Dv:Dv + 1, :]
        o_ref[:, g * Dv:(g + 1) * Dv] = out_t.T.astype(o_ref.dtype)


def _mla_attention(q_cat, k_cat, v_t, B, S, tq):
    H = MLA_HEADS
    T = q_cat.shape[0]
    G = 8
    nq = S // tq
    once = pl.Buffered(1)
    return pl.pallas_call(
        functools.partial(_mla_attn_kernel, tq=tq, heads=G),
        grid=(B, H // G, nq),
        in_specs=[pl.BlockSpec((tq, G * QPAD), lambda b, h, i: (b * nq + i, h)),
                  pl.BlockSpec((S, G * QPAD), lambda b, h, i: (b, h), pipeline_mode=once),
                  pl.BlockSpec((nq, G * VT_ROWS, tq), lambda b, h, i: (b, h, 0), pipeline_mode=once)],
        out_specs=pl.BlockSpec((tq, G * V_HEAD_DIM), lambda b, h, i: (b * nq + i, h)),
        out_shape=jax.ShapeDtypeStruct((T, H * V_HEAD_DIM), BF16),
        scratch_shapes=[pltpu.VMEM((G, 1, tq), F32),
                        pltpu.VMEM((G, VT_ROWS, tq), F32), pltpu.VMEM((G, tq, tq), F32)],
        compiler_params=_cparams("parallel", "parallel", "arbitrary"),
        name="mla_attention",
    )(q_cat, k_cat, v_t)


def _rot_cols(w):
    half = QK_ROPE_DIM // 2
    return jnp.concatenate([-w[..., half:], w[..., :half]], axis=-1)


def _q_weights(w_uq):
    R = w_uq.shape[0]
    w = w_uq.reshape(R, MLA_HEADS, QK_NOPE_DIM + QK_ROPE_DIM)
    wn, wr = w[..., :QK_NOPE_DIM], w[..., QK_NOPE_DIM:]
    return jnp.concatenate([wn, wr, _rot_cols(wr)], axis=-1).transpose(1, 0, 2).astype(BF16)


def kernel(x, c, positions, w_mod, b_mod, g_mix, g_ffn, w_qkv_a, w_o_a, g_kv_in, w_mod_kv, b_mod_kv,
           w_dkv, g_ckv, w_uk, w_uv, w_dq_b, g_cq_b, w_uq_b, w_o_b, router_w, router_b, w_gu, b_gu,
           w_down, b_down, g_final):
    B, S, D = x.shape
    T = B * S
    depth = w_mod.shape[0]
    n_a = w_qkv_a.shape[0]
    xt = x.reshape(T, D)

    mod = _adaln(c, w_mod, b_mod)
    mod_kv = _adaln(c, w_mod_kv[None], b_mod_kv[None])[0]

    half = QK_ROPE_DIM // 2
    inv_freq = ROPE_BASE ** (-jnp.arange(half, dtype=F32) / half)
    ang = positions.astype(F32).reshape(T, 1) * inv_freq
    cos_f, sin_f = _cossin(ang.reshape(T * half // LANES, LANES))
    cos = cos_f.reshape(T, half)
    sin = sin_f.reshape(T, half)
    z64 = jnp.zeros((T, QK_ROPE_DIM), F32)
    cos_k = jnp.concatenate([cos, cos, z64], axis=1)
    sin_k = jnp.concatenate([sin, sin, z64], axis=1)
    qs = LOG2_E * (QK_NOPE_DIM + QK_ROPE_DIM) ** -0.5
    cos_q = jnp.concatenate([jnp.ones((T, QK_NOPE_DIM), F32), cos_k], axis=1) * qs
    sin_q = jnp.concatenate([jnp.zeros((T, QK_NOPE_DIM), F32), sin_k], axis=1) * qs

    tq_mla = min(S, 256)
    w_gu_b = w_gu.astype(BF16)
    w_down_b = w_down.astype(BF16)

    shared = None
    for layer in range(depth):
        m = mod[layer]
        shift_a, scale_a, gate_a, shift_m, scale_m, gate_m = [m[:, i * D:(i + 1) * D] for i in range(6)]
        if layer < n_a:
            qkv = _nm_matmul(xt, g_mix[layer], shift_a, scale_a, w_qkv_a[layer].astype(BF16), S,
                             q_cols=SB_HEADS * SB_HEAD_DIM, q_scale=LOG2_E * SB_HEAD_DIM ** -0.5)
            o = _sb_attention(qkv, B, S)
            w_o = w_o_a[layer]
        else:
            i = layer - n_a
            k_cat, v_b = shared
            cq = _dq(xt, g_mix[layer], shift_a, scale_a, w_dq_b[i].astype(BF16), g_cq_b[i], S)
            q_cat = _uq_rope(cq, _q_weights(w_uq_b[i]), cos_q, sin_q)
            o = _mla_attention(q_cat, k_cat, v_b, B, S, tq_mla)
            w_o = w_o_b[i]
        xt, h, logits_t = _oproj(o, w_o.astype(BF16), xt, gate_a, g_ffn[layer], shift_m, scale_m,
                                 router_w[layer], router_b[layer], S)
        xt = _moe(h, logits_t, xt, gate_m, g_final, w_gu_b, b_gu, w_down_b, b_down, layer, S,
                  final=(layer == depth - 1))
        if layer == n_a - 1:
            w_ext = jnp.concatenate([w_dkv, _rot_cols(w_dkv[:, KV_LORA_RANK:])], axis=1).astype(BF16)
            ckv, k_rope = _kv_latent(xt, g_kv_in, mod_kv[:, :D], mod_kv[:, D:], w_ext, g_ckv,
                                     cos_k, sin_k, S)
            shared = _kv_up(ckv, w_uk.astype(BF16), w_uv.T.astype(BF16), k_rope, tq_mla)
    return xt.reshape(B, S, D)
```

```python
import functools

import jax
import jax.numpy as jnp
from jax import lax
from jax.experimental import pallas as pl
from jax.experimental.pallas import tpu as pltpu

F32 = jnp.float32
BF16 = jnp.bfloat16
I32 = jnp.int32
U32 = jnp.uint32

CHUNK = 64
SB_HEADS = 16
SB_HEAD_DIM = 128
MLA_HEADS = 16
QK_NOPE_DIM = 128
QK_ROPE_DIM = 64
V_HEAD_DIM = 128
KV_LORA_RANK = 512
ROPE_BASE = 10000.0
N_EXPERTS = 32
TOP_K = 4
D_FF_EXPERT = 768
SWIGLU_ALPHA = 1.702
SWIGLU_LIMIT = 7.0
RMS_EPS = 1e-6

V7X_VMEM_LIMIT_BYTES = 52 * 1024 * 1024
LANES = 128
QPAD = 256
VT_ROWS = V_HEAD_DIM + 16

NEG_INF = float("-inf")
LOG2_E = 1.4426950408889634
SB_UNDERFLOW_LOG2 = -150.0


def _cparams(*sem):
    return pltpu.CompilerParams(dimension_semantics=sem, vmem_limit_bytes=V7X_VMEM_LIMIT_BYTES)


def _split_bf16(a):
    hi = a.astype(BF16)
    lo = (a - hi.astype(F32)).astype(BF16)
    return hi, lo


def _dot(a, b):
    return jnp.dot(a, b, preferred_element_type=F32)


def _dot_nt(a, b):
    return lax.dot_general(a, b, (((1,), (1,)), ((), ())), preferred_element_type=F32)


def _pack_bf16_pair(lo, hi):
    lo_bits = lax.bitcast_convert_type(lo.astype(BF16).astype(F32), U32) >> 16
    hi_bits = lax.bitcast_convert_type(hi.astype(BF16).astype(F32), U32) & jnp.uint32(0xFFFF0000)
    return hi_bits | lo_bits


def _unpack_bf16_pair(w):
    lo = lax.bitcast_convert_type(w << 16, F32)
    hi = lax.bitcast_convert_type(w & jnp.uint32(0xFFFF0000), F32)
    return lo, hi


def _rms(x, g):
    ms = jnp.mean(x * x, axis=-1, keepdims=True)
    return x * lax.rsqrt(ms + RMS_EPS) * g


def _norm_mod(x, g, shift, scale):
    return _rms(x, g) * (1.0 + scale) + shift


def _adaln_kernel(c_ref, w_ref, b_ref, o_ref):
    c = c_ref[...]
    s = c * (1.0 / (1.0 + jnp.exp(-c)))
    s_hi, s_lo = _split_bf16(s)
    w_hi, w_lo = _split_bf16(w_ref[0])
    o_ref[0] = _dot(s_hi, w_hi) + _dot(s_lo, w_hi) + _dot(s_hi, w_lo) + b_ref[0]


def _adaln(c, w, b):
    L, D, N = w.shape
    B = c.shape[0]
    tn = 1024
    return pl.pallas_call(
        _adaln_kernel,
        grid=(L, N // tn),
        in_specs=[pl.BlockSpec((B, D), lambda l, j: (0, 0)),
                  pl.BlockSpec((1, D, tn), lambda l, j: (l, 0, j)),
                  pl.BlockSpec((1, 1, tn), lambda l, j: (l, 0, j))],
        out_specs=pl.BlockSpec((1, B, tn), lambda l, j: (l, 0, j)),
        out_shape=jax.ShapeDtypeStruct((L, B, N), F32),
        compiler_params=_cparams("parallel", "parallel"),
        name="adaln",
    )(c, w, b.reshape(L, 1, N))


def _cossin_kernel(a_ref, c_ref, s_ref):
    a = a_ref[...]
    c_ref[...] = jnp.cos(a)
    s_ref[...] = jnp.sin(a)


def _cossin(ang):
    R = ang.shape[0]
    tr = min(R, 512)
    spec = pl.BlockSpec((tr, LANES), lambda i: (i, 0))
    return pl.pallas_call(
        _cossin_kernel, grid=(R // tr,), in_specs=[spec], out_specs=[spec, spec],
        out_shape=[jax.ShapeDtypeStruct(ang.shape, F32)] * 2,
        compiler_params=_cparams("parallel"), name="cossin",
    )(ang)


def _nm_matmul_kernel(x_ref, g_ref, sh_ref, sc_ref, w_ref, o_ref, h_ref, *, n_scaled, col_scale):
    j = pl.program_id(1)

    @pl.when(j == 0)
    def _():
        h_ref[...] = _norm_mod(x_ref[...], g_ref[...], sh_ref[0], sc_ref[0]).astype(BF16)

    acc = _dot(h_ref[...], w_ref[...])
    s = jnp.where(j < n_scaled, col_scale, 1.0).astype(F32)
    o_ref[...] = (acc * s).astype(o_ref.dtype)


def _nm_matmul(x, g, shift, scale, w, S, *, q_cols=0, q_scale=1.0):
    T, D = x.shape
    N = w.shape[1]
    tm = min(S, 512)
    tn = 1024
    per_b = S // tm
    B = T // S
    mod_spec = pl.BlockSpec((1, 1, D), lambda i, j: (i // per_b, 0, 0))
    return pl.pallas_call(
        functools.partial(_nm_matmul_kernel, n_scaled=q_cols // tn, col_scale=q_scale),
        grid=(T // tm, N // tn),
        in_specs=[pl.BlockSpec((tm, D), lambda i, j: (i, 0)),
                  pl.BlockSpec((1, D), lambda i, j: (0, 0)),
                  mod_spec, mod_spec,
                  pl.BlockSpec((D, tn), lambda i, j: (0, j))],
        out_specs=pl.BlockSpec((tm, tn), lambda i, j: (i, j)),
        out_shape=jax.ShapeDtypeStruct((T, N), BF16),
        scratch_shapes=[pltpu.VMEM((tm, D), BF16)],
        compiler_params=_cparams("parallel", "arbitrary"),
        name="norm_mod_matmul",
    )(x, g.reshape(1, D), shift.reshape(B, 1, D), scale.reshape(B, 1, D), w)


def _sb_attn_kernel(q_ref, k_ref, v_ref, tri_ref, o_ref, acc_ref, run_ref, *, tq, tk, heads):
    qi = pl.program_id(2)
    Dh = SB_HEAD_DIM
    tri = tri_ref[...]
    n_kb = ((qi + 1) * tq) // tk
    n_diag = tq // tk
    t_loc = lax.broadcasted_iota(I32, (tq, tk), 0)
    s_loc = lax.broadcasted_iota(I32, (tq, tk), 1)
    acc_ref[...] = jnp.zeros_like(acc_ref)
    run_ref[...] = jnp.zeros_like(run_ref)

    def block(j, mask):
        ks = pl.multiple_of(j * tk, tk)
        heads_cols = [slice(g * Dh, (g + 1) * Dh) for g in range(heads)]
        zs = [_dot_nt(q_ref[:, c], k_ref[pl.ds(ks, tk), c]) for c in heads_cols]
        ps = []
        for z in zs:
            lk = jnp.minimum(-z, 0.0) - jnp.log2(1.0 + jnp.exp2(-jnp.abs(z)))
            if mask is not None:
                lk = jnp.where(mask, lk, 0.0)
            ps.append(_dot(lk.astype(BF16), tri))
        worst = None
        for g in range(heads):
            run = run_ref[g]
            a = jnp.exp2(zs[g] + ps[g][:, :tk] + run)
            if mask is not None:
                a = jnp.where(mask, a, 0.0)
            acc_ref[g] += _dot(a.astype(BF16), v_ref[pl.ds(ks, tk), heads_cols[g]])
            run = run + ps[g][:, tk:]
            run_ref[g] = run
            m = jnp.max(run)
            worst = m if worst is None else jnp.maximum(worst, m)
        return worst

    for d in range(n_diag):
        j = n_kb - 1 - d
        worst = block(j, (j * tk - qi * tq + s_loc) < t_loc)

    def cond(c):
        j, worst = c
        return jnp.logical_and(j >= 0, worst > SB_UNDERFLOW_LOG2)

    def body(c):
        j, _ = c
        return j - 1, block(j, None)

    lax.while_loop(cond, body, (n_kb - n_diag - 1, worst))
    for g in range(heads):
        o_ref[:, g * Dh:(g + 1) * Dh] = acc_ref[g].astype(o_ref.dtype)


def _sb_attention(qkv, B, S):
    H, Dh = SB_HEADS, SB_HEAD_DIM
    T = qkv.shape[0]
    tk = 128
    tq = min(S, 128)
    G = 8
    nq = S // tq
    hb = H // G
    jj = jnp.arange(tk)[:, None]
    ss = jnp.arange(tk + LANES)[None, :]
    tri = ((jj >= ss) | (ss >= tk)).astype(BF16)
    return pl.pallas_call(
        functools.partial(_sb_attn_kernel, tq=tq, tk=tk, heads=G),
        grid=(B, hb, nq),
        in_specs=[pl.BlockSpec((tq, G * Dh), lambda b, h, i: (b * nq + i, h)),
                  pl.BlockSpec((S, G * Dh), lambda b, h, i: (b, hb + h)),
                  pl.BlockSpec((S, G * Dh), lambda b, h, i: (b, 2 * hb + h)),
                  pl.BlockSpec((tk, tk + LANES), lambda b, h, i: (0, 0))],
        out_specs=pl.BlockSpec((tq, G * Dh), lambda b, h, i: (b * nq + i, h)),
        out_shape=jax.ShapeDtypeStruct((T, H * Dh), BF16),
        scratch_shapes=[pltpu.VMEM((G, tq, Dh), F32), pltpu.VMEM((G, tq, LANES), F32)],
        compiler_params=_cparams("parallel", "parallel", "arbitrary"),
        name="sb_attention",
    )(qkv, qkv, qkv, tri)


def _oproj_kernel(o_ref, w_ref, x_ref, ga_ref, g_ref, sh_ref, sc_ref, rwt_ref, rb_ref,
                  xo_ref, h_ref, lg_ref):
    mix = _dot(o_ref[...], w_ref[...])
    xn = x_ref[...] + ga_ref[0] * mix
    xo_ref[...] = xn
    h = _norm_mod(xn, g_ref[...], sh_ref[0], sc_ref[0])
    half = h.shape[1] // 2
    h_ref[...] = _pack_bf16_pair(h[:, :half], h[:, half:])
    h_hi, h_lo = _split_bf16(h)
    r_hi, r_lo = _split_bf16(rwt_ref[...])
    lg_ref[...] = _dot_nt(r_hi, h_hi) + _dot_nt(r_lo, h_hi) + _dot_nt(r_hi, h_lo) + rb_ref[...]


def _oproj(o, w, x, gate, g, shift, scale, router_w, router_b, S):
    T, D = x.shape
    E = router_w.shape[1]
    B = T // S
    tm = min(S, 512)
    per_b = S // tm
    row = pl.BlockSpec((tm, D), lambda i: (i, 0))
    mod = pl.BlockSpec((1, 1, D), lambda i: (i // per_b, 0, 0))
    return pl.pallas_call(
        _oproj_kernel,
        grid=(T // tm,),
        in_specs=[row,
                  pl.BlockSpec((D, D), lambda i: (0, 0), pipeline_mode=pl.Buffered(1)),
                  row, mod,
                  pl.BlockSpec((1, D), lambda i: (0, 0)),
                  mod, mod,
                  pl.BlockSpec((E, D), lambda i: (0, 0)),
                  pl.BlockSpec((E, 1), lambda i: (0, 0))],
        out_specs=[row, pl.BlockSpec((tm, D // 2), lambda i: (i, 0)), pl.BlockSpec((E, tm), lambda i: (0, i))],
        out_shape=[jax.ShapeDtypeStruct((T, D), F32),
                   jax.ShapeDtypeStruct((T, D // 2), U32),
                   jax.ShapeDtypeStruct((E, T), F32)],
        compiler_params=_cparams("parallel"),
        name="oproj_residual_router",
    )(o, w, x, gate.reshape(B, 1, D), g.reshape(1, D), shift.reshape(B, 1, D),
      scale.reshape(B, 1, D), router_w.T, router_b.reshape(E, 1))


def _route_kernel(lg_ref, tri_ref, idx_ref, gate_ref, rank_ref, cnt_ref, carry_ref):
    i = pl.program_id(0)

    @pl.when(i == 0)
    def _():
        carry_ref[...] = jnp.zeros_like(carry_ref)

    work = lg_ref[...]
    E, tt = work.shape
    eio = lax.broadcasted_iota(I32, (E, tt), 0)
    vals, onehots = [], []
    for k in range(TOP_K):
        m = jnp.max(work, axis=0, keepdims=True)
        ix = jnp.min(jnp.where(work == m, eio, E), axis=0, keepdims=True)
        oh = eio == ix
        work = jnp.where(oh, NEG_INF, work)
        vals.append(m)
        onehots.append(oh)
        idx_ref[k:k + 1, :] = ix
    es = [jnp.exp(v - vals[0]) for v in vals]
    den = es[0] + es[1] + es[2] + es[3]
    for k in range(TOP_K):
        gate_ref[k:k + 1, :] = es[k] / den
    sel = jnp.zeros((E, tt), F32)
    for oh in onehots:
        sel = sel + jnp.where(oh, 1.0, 0.0)
    cnt = _dot(sel.astype(BF16), tri_ref[...]) + carry_ref[:, 0:1]
    for k in range(TOP_K):
        r = jnp.sum(jnp.where(onehots[k], cnt - 1.0, 0.0), axis=0, keepdims=True)
        rank_ref[k:k + 1, :] = r.astype(I32)
    last = jnp.broadcast_to(cnt[:, tt - 1:tt], (E, LANES))
    carry_ref[...] = last
    cnt_ref[...] = last


def _route(logits_t):
    E, T = logits_t.shape
    tt = min(T, 512)
    tri = (jnp.arange(tt)[:, None] <= jnp.arange(tt)[None, :]).astype(BF16)
    tok = pl.BlockSpec((TOP_K, tt), lambda i: (0, i))
    return pl.pallas_call(
        _route_kernel,
        grid=(T // tt,),
        in_specs=[pl.BlockSpec((E, tt), lambda i: (0, i)),
                  pl.BlockSpec((tt, tt), lambda i: (0, 0))],
        out_specs=[tok, tok, tok, pl.BlockSpec((E, LANES), lambda i: (0, 0))],
        out_shape=[jax.ShapeDtypeStruct((TOP_K, T), I32),
                   jax.ShapeDtypeStruct((TOP_K, T), F32),
                   jax.ShapeDtypeStruct((TOP_K, T), I32),
                   jax.ShapeDtypeStruct((E, LANES), F32)],
        scratch_shapes=[pltpu.VMEM((E, LANES), F32)],
        compiler_params=_cparams("arbitrary"),
        name="route_topk_rank",
    )(logits_t, tri)


def _dest_kernel(idx_ref, rank_ref, ps_ref, dest_ref):
    E = ps_ref.shape[0]
    tt = idx_ref.shape[1]
    eio = lax.broadcasted_iota(I32, (E, tt), 0)
    ps = ps_ref[...]
    for k in range(TOP_K):
        oh = eio == idx_ref[k:k + 1, :]
        base = jnp.sum(jnp.where(oh, ps, 0), axis=0, keepdims=True)
        dest_ref[k:k + 1, :] = base + rank_ref[k:k + 1, :]


def _dest(idx_t, rank_t, pstart):
    T = idx_t.shape[1]
    E = pstart.shape[0]
    tt = min(T, 2048)
    tok = pl.BlockSpec((TOP_K, tt), lambda i: (0, i))
    return pl.pallas_call(
        _dest_kernel, grid=(T // tt,),
        in_specs=[tok, tok, pl.BlockSpec((E, 1), lambda i: (0, 0))],
        out_specs=tok,
        out_shape=jax.ShapeDtypeStruct((TOP_K, T), I32),
        compiler_params=_cparams("parallel"), name="route_dest",
    )(idx_t, rank_t, pstart.reshape(E, 1))


ROWS_PER_ISSUE_TRIP = 4


def _row_copy(src, s_row, dst, d_row, sem):
    return pltpu.make_async_copy(src.at[pl.ds(s_row, 1)], dst.at[pl.ds(d_row, 1)], sem)


def _dispatch_kernel(pend_ref, padded_ref, dest_ref, h_ref, xs_hbm, zbuf, sem, zsem, *, tt, tm):
    @pl.when(pl.program_id(0) == 0)
    def _():
        zbuf[...] = jnp.zeros_like(zbuf)

        def zero_copy(e):
            start = pl.multiple_of(pend_ref[e] - tm, tm)
            return pltpu.make_async_copy(zbuf, xs_hbm.at[pl.ds(start, tm)], zsem)

        def start(e, c):
            @pl.when(padded_ref[e] > 0)
            def _():
                zero_copy(e).start()
            return c

        def wait(e, c):
            @pl.when(padded_ref[e] > 0)
            def _():
                zero_copy(e).wait()
            return c

        lax.fori_loop(0, N_EXPERTS, start, 0)
        lax.fori_loop(0, N_EXPERTS, wait, 0)

        def tail_copy(b):
            return pltpu.make_async_copy(zbuf, xs_hbm.at[pl.ds(pl.multiple_of(b * tm, tm), tm)], zsem)

        def tail_start(b, c):
            tail_copy(b).start()
            return c

        def tail_wait(b, c):
            tail_copy(b).wait()
            return c

        first_unused = pend_ref[N_EXPERTS - 1] // tm
        lax.fori_loop(first_unused, xs_hbm.shape[0] // tm, tail_start, 0)
        lax.fori_loop(first_unused, xs_hbm.shape[0] // tm, tail_wait, 0)

    def issue(rr, c):
        for u in range(ROWS_PER_ISSUE_TRIP):
            r = pl.multiple_of(rr * ROWS_PER_ISSUE_TRIP, ROWS_PER_ISSUE_TRIP) + u
            for k in range(TOP_K):
                _row_copy(h_ref, r, xs_hbm, dest_ref[r * TOP_K + k], sem).start(priority=k % 2)
        return c

    lax.fori_loop(0, tt // ROWS_PER_ISSUE_TRIP, issue, 0)

    def drain(r, c):
        for k in range(TOP_K):
            _row_copy(h_ref, 0, xs_hbm, 0, sem).wait()
        return c

    lax.fori_loop(0, tt, drain, 0)


def _dispatch(pends, padded, dest_flat, h, n_rows, tm):
    T, D = h.shape
    tt = min(T, 256)
    return pl.pallas_call(
        functools.partial(_dispatch_kernel, tt=tt, tm=tm),
        grid_spec=pltpu.PrefetchScalarGridSpec(
            num_scalar_prefetch=2,
            grid=(T // tt,),
            in_specs=[pl.BlockSpec((tt * TOP_K,), lambda i, pe, pa: (i,), memory_space=pltpu.SMEM),
                      pl.BlockSpec((tt, D), lambda i, pe, pa: (i, 0))],
            out_specs=pl.BlockSpec(memory_space=pl.ANY),
            scratch_shapes=[pltpu.VMEM((tm, D), h.dtype), pltpu.SemaphoreType.DMA(()),
                            pltpu.SemaphoreType.DMA(())],
        ),
        out_shape=jax.ShapeDtypeStruct((n_rows, D), h.dtype),
        compiler_params=_cparams("arbitrary"),
        name="moe_dispatch",
    )(pends, padded, dest_flat, h)


def _gmm_kernel(be_ref, nu_ref, xs_ref, wgu_ref, bgu_ref, wd_ref, bd_ref, y_ref):
    del be_ref
    i = pl.program_id(0)
    F = D_FF_EXPERT

    @pl.when(i < nu_ref[0])
    def _():
        x_lo, x_hi = _unpack_bf16_pair(xs_ref[...])
        half = x_lo.shape[1]
        gu = (_dot(x_lo.astype(BF16), wgu_ref[0, 0, :half, :])
              + _dot(x_hi.astype(BF16), wgu_ref[0, 0, half:, :]) + bgu_ref[0, 0])
        g = jnp.minimum(gu[:, :F], SWIGLU_LIMIT)
        u = jnp.clip(gu[:, F:], -SWIGLU_LIMIT, SWIGLU_LIMIT)
        a = g * (1.0 / (1.0 + jnp.exp(-SWIGLU_ALPHA * g))) * (u + 1.0)
        y = _dot(a.astype(BF16), wd_ref[0, 0]) + bd_ref[0, 0]
        y_ref[...] = _pack_bf16_pair(y[:, :half], y[:, half:])

    @pl.when(i >= nu_ref[0])
    def _():
        y_ref[...] = jnp.zeros_like(y_ref)


def _gmm(block_expert, n_used, xs, w_gu, b_gu, w_down, b_down, layer, tm):
    n_rows, Dp = xs.shape
    L, E, D, F2 = w_gu.shape
    F = F2 // 2
    n_blocks = n_rows // tm

    def row_map(i, be, nu):
        return (jnp.minimum(i, nu[0] - 1), 0)

    def w_map(i, be, nu):
        return (layer, be[i], 0, 0)

    return pl.pallas_call(
        _gmm_kernel,
        grid_spec=pltpu.PrefetchScalarGridSpec(
            num_scalar_prefetch=2,
            grid=(n_blocks,),
            in_specs=[pl.BlockSpec((tm, Dp), row_map),
                      pl.BlockSpec((1, 1, D, F2), w_map),
                      pl.BlockSpec((1, 1, 1, F2), w_map),
                      pl.BlockSpec((1, 1, F, D), w_map),
                      pl.BlockSpec((1, 1, 1, D), w_map)],
            out_specs=pl.BlockSpec((tm, Dp), lambda i, be, nu: (i, 0)),
        ),
        out_shape=jax.ShapeDtypeStruct((n_rows, Dp), U32),
        compiler_params=_cparams("arbitrary"),
        name="moe_expert_mlp",
    )(block_expert, n_used, xs, w_gu, b_gu.reshape(L, E, 1, F2), w_down, b_down.reshape(L, E, 1, D))


def _combine_kernel(dest_ref, dest_next_ref, gates_ref, x_ref, gm_ref, gf_ref, y_hbm, o_ref, ybuf, sems,
                    *, tt, final):
    i = pl.program_id(0)
    slot = lax.rem(i, 2)

    def gather(dref, s):
        def issue(rr, c):
            for u in range(ROWS_PER_ISSUE_TRIP):
                r = pl.multiple_of(rr * ROWS_PER_ISSUE_TRIP, ROWS_PER_ISSUE_TRIP) + u
                for k in range(TOP_K):
                    _row_copy(y_hbm, dref[r * TOP_K + k], ybuf.at[s, k], r, sems.at[s]).start(priority=k % 2)
            return c

        lax.fori_loop(0, tt // ROWS_PER_ISSUE_TRIP, issue, 0)

    @pl.when(i == 0)
    def _():
        gather(dest_ref, 0)

    @pl.when(i + 1 < pl.num_programs(0))
    def _():
        gather(dest_next_ref, 1 - slot)

    def drain(r, c):
        for k in range(TOP_K):
            _row_copy(y_hbm, 0, ybuf.at[slot, k], 0, sems.at[slot]).wait()
        return c

    lax.fori_loop(0, tt, drain, 0)

    g = gates_ref[...]
    half = ybuf.shape[-1]
    moe_lo = moe_hi = None
    for k in range(TOP_K):
        lo, hi = _unpack_bf16_pair(ybuf[slot, k])
        gk = g[:, k:k + 1]
        moe_lo = gk * lo if moe_lo is None else moe_lo + gk * lo
        moe_hi = gk * hi if moe_hi is None else moe_hi + gk * hi
    gm = gm_ref[0]
    x_lo = x_ref[:, :half] + gm[:, :half] * moe_lo
    x_hi = x_ref[:, half:] + gm[:, half:] * moe_hi
    if final:
        ms = (jnp.sum(x_lo * x_lo, axis=-1, keepdims=True)
              + jnp.sum(x_hi * x_hi, axis=-1, keepdims=True)) / (2 * half)
        inv = lax.rsqrt(ms + RMS_EPS)
        x_lo = x_lo * inv * gf_ref[:, :half]
        x_hi = x_hi * inv * gf_ref[:, half:]
    o_ref[:, :half] = x_lo
    o_ref[:, half:] = x_hi


def _combine(dest_flat, gates, x, gate_m, g_final, y, S, final):
    T, D = x.shape
    B = T // S
    tt = min(S, 256)
    per_b = S // tt
    n_tiles = T // tt
    row = pl.BlockSpec((tt, D), lambda i: (i, 0))
    return pl.pallas_call(
        functools.partial(_combine_kernel, tt=tt, final=final),
        grid=(n_tiles,),
        in_specs=[pl.BlockSpec((tt * TOP_K,), lambda i: (i,), memory_space=pltpu.SMEM),
                  pl.BlockSpec((tt * TOP_K,), lambda i: (jnp.minimum(i + 1, n_tiles - 1),),
                               memory_space=pltpu.SMEM),
                  pl.BlockSpec((tt, TOP_K), lambda i: (i, 0)),
                  row,
                  pl.BlockSpec((1, 1, D), lambda i: (i // per_b, 0, 0)),
                  pl.BlockSpec((1, D), lambda i: (0, 0)),
                  pl.BlockSpec(memory_space=pl.ANY)],
        out_specs=row,
        out_shape=jax.ShapeDtypeStruct((T, D), F32),
        scratch_shapes=[pltpu.VMEM((2, TOP_K, tt, D // 2), U32), pltpu.SemaphoreType.DMA((2,))],
        compiler_params=_cparams("arbitrary"),
        name="moe_combine",
    )(dest_flat, dest_flat, gates, x, gate_m.reshape(B, 1, D), g_final.reshape(1, D), y)


def _moe(h, logits_t, x, gate_m, g_final, w_gu, b_gu, w_down, b_down, layer, S, final):
    T, D = h.shape
    E = N_EXPERTS
    tm = 512
    idx_t, gates_t, rank_t, cnt = _route(logits_t)
    counts = cnt[:, 0].astype(I32)
    padded = (counts + tm - 1) // tm * tm
    pends = jnp.cumsum(padded)
    pstart = pends - padded
    n_rows = T * TOP_K + E * tm
    n_blocks = n_rows // tm
    block_start = jnp.arange(n_blocks, dtype=I32) * tm
    block_expert = jnp.minimum(jnp.sum((pends[None, :] <= block_start[:, None]).astype(I32), axis=1), E - 1)
    n_used = (pends[-1:] // tm).astype(I32)
    dest_flat = _dest(idx_t, rank_t, pstart).T.reshape(-1)
    xs = _dispatch(pends, padded, dest_flat, h, n_rows, tm)
    y = _gmm(block_expert, n_used, xs, w_gu, b_gu, w_down, b_down, layer, tm)
    return _combine(dest_flat, gates_t.T, x, gate_m, g_final, y, S, final)


def _dq_kernel(x_ref, g_ref, sh_ref, sc_ref, w_ref, gcq_ref, o_ref):
    h = _norm_mod(x_ref[...], g_ref[...], sh_ref[0], sc_ref[0]).astype(BF16)
    o_ref[...] = _rms(_dot(h, w_ref[...]), gcq_ref[...]).astype(o_ref.dtype)


def _dq(x, g, shift, scale, w, gcq, S):
    T, D = x.shape
    R = w.shape[1]
    B = T // S
    tm = min(S, 512)
    per_b = S // tm
    mod = pl.BlockSpec((1, 1, D), lambda i: (i // per_b, 0, 0))
    return pl.pallas_call(
        _dq_kernel, grid=(T // tm,),
        in_specs=[pl.BlockSpec((tm, D), lambda i: (i, 0)),
                  pl.BlockSpec((1, D), lambda i: (0, 0)), mod, mod,
                  pl.BlockSpec((D, R), lambda i: (0, 0)),
                  pl.BlockSpec((1, R), lambda i: (0, 0))],
        out_specs=pl.BlockSpec((tm, R), lambda i: (i, 0)),
        out_shape=jax.ShapeDtypeStruct((T, R), BF16),
        compiler_params=_cparams("parallel"), name="q_down_norm",
    )(x, g.reshape(1, D), shift.reshape(B, 1, D), scale.reshape(B, 1, D), w, gcq.reshape(1, R))


def _uq_rope_kernel(cq_ref, w_ref, cos_ref, sin_ref, o_ref):
    cq = cq_ref[...]
    cos = cos_ref[...]
    sin = sin_ref[...]
    for h in range(w_ref.shape[0]):
        acc = _dot(cq, w_ref[h])
        q = acc * cos + pltpu.roll(acc, QPAD - QK_ROPE_DIM, 1) * sin
        o_ref[:, h * QPAD:(h + 1) * QPAD] = q.astype(o_ref.dtype)


def _uq_rope(cq, wq, cos_p, sin_p):
    T, R = cq.shape
    H = wq.shape[0]
    tm = min(T, 512)
    tab = pl.BlockSpec((tm, QPAD), lambda i: (i, 0))
    return pl.pallas_call(
        _uq_rope_kernel, grid=(T // tm,),
        in_specs=[pl.BlockSpec((tm, R), lambda i: (i, 0)),
                  pl.BlockSpec((H, R, QPAD), lambda i: (0, 0, 0)), tab, tab],
        out_specs=pl.BlockSpec((tm, H * QPAD), lambda i: (i, 0)),
        out_shape=jax.ShapeDtypeStruct((T, H * QPAD), BF16),
        compiler_params=_cparams("parallel"), name="q_up_rope",
    )(cq, wq, cos_p, sin_p)


def _kv_latent_kernel(x_ref, g_ref, sh_ref, sc_ref, w_ref, gc_ref, cos_ref, sin_ref, ckv_ref, kr_ref):
    h = _norm_mod(x_ref[...], g_ref[...], sh_ref[0], sc_ref[0]).astype(BF16)
    lat = _dot(h, w_ref[...])
    ckv_ref[...] = _rms(lat[:, :KV_LORA_RANK], gc_ref[...]).astype(ckv_ref.dtype)
    t = lat[:, KV_LORA_RANK:]
    kr_ref[...] = (t * cos_ref[...] + pltpu.roll(t, QK_ROPE_DIM, 1) * sin_ref[...]).astype(kr_ref.dtype)


def _kv_latent(x, g, shift, scale, w_ext, g_ckv, cos_k, sin_k, S):
    T, D = x.shape
    B = T // S
    N = w_ext.shape[1]
    tm = min(S, 512)
    per_b = S // tm
    mod = pl.BlockSpec((1, 1, D), lambda i: (i // per_b, 0, 0))
    tab = pl.BlockSpec((tm, LANES), lambda i: (i, 0))
    return pl.pallas_call(
        _kv_latent_kernel, grid=(T // tm,),
        in_specs=[pl.BlockSpec((tm, D), lambda i: (i, 0)),
                  pl.BlockSpec((1, D), lambda i: (0, 0)), mod, mod,
                  pl.BlockSpec((D, N), lambda i: (0, 0)),
                  pl.BlockSpec((1, KV_LORA_RANK), lambda i: (0, 0)), tab, tab],
        out_specs=[pl.BlockSpec((tm, KV_LORA_RANK), lambda i: (i, 0)), tab],
        out_shape=[jax.ShapeDtypeStruct((T, KV_LORA_RANK), BF16),
                   jax.ShapeDtypeStruct((T, LANES), BF16)],
        compiler_params=_cparams("parallel"), name="kv_latent",
    )(x, g.reshape(1, D), shift.reshape(B, 1, D), scale.reshape(B, 1, D), w_ext,
      g_ckv.reshape(1, KV_LORA_RANK), cos_k, sin_k)


def _kv_up_kernel(c_ref, wk_ref, wvt_ref, kr_ref, k_ref, vt_ref, *, tk):
    c = c_ref[...]
    kn = _dot(c, wk_ref[...]).astype(k_ref.dtype)
    kr = kr_ref[...]
    for h in range(MLA_HEADS):
        k_ref[:, h * QPAD:h * QPAD + QK_NOPE_DIM] = kn[:, h * QK_NOPE_DIM:(h + 1) * QK_NOPE_DIM]
        k_ref[:, h * QPAD + QK_NOPE_DIM:(h + 1) * QPAD] = kr
    vt = _dot_nt(wvt_ref[...], c).astype(vt_ref.dtype)
    ones = jnp.ones((VT_ROWS - V_HEAD_DIM, tk), vt_ref.dtype)
    for j in range(vt_ref.shape[0]):
        for h in range(MLA_HEADS):
            vt_ref[j, h * VT_ROWS:h * VT_ROWS + V_HEAD_DIM, :] = vt[h * V_HEAD_DIM:(h + 1) * V_HEAD_DIM,
                                                                    j * tk:(j + 1) * tk]
            vt_ref[j, h * VT_ROWS + V_HEAD_DIM:(h + 1) * VT_ROWS, :] = ones


def _kv_up(ckv, w_uk, w_uv_t, k_rope, tk):
    T, R = ckv.shape
    H = MLA_HEADS
    tm = min(T, 512)
    return pl.pallas_call(
        functools.partial(_kv_up_kernel, tk=tk), grid=(T // tm,),
        in_specs=[pl.BlockSpec((tm, R), lambda i: (i, 0)),
                  pl.BlockSpec((R, H * QK_NOPE_DIM), lambda i: (0, 0)),
                  pl.BlockSpec((H * V_HEAD_DIM, R), lambda i: (0, 0)),
                  pl.BlockSpec((tm, LANES), lambda i: (i, 0))],
        out_specs=[pl.BlockSpec((tm, H * QPAD), lambda i: (i, 0)),
                   pl.BlockSpec((tm // tk, H * VT_ROWS, tk), lambda i: (i, 0, 0))],
        out_shape=[jax.ShapeDtypeStruct((T, H * QPAD), BF16),
                   jax.ShapeDtypeStruct((T // tk, H * VT_ROWS, tk), BF16)],
        compiler_params=_cparams("parallel"), name="kv_up",
    )(ckv, w_uk, w_uv_t, k_rope)


def _mla_attn_kernel(q_ref, k_ref, vt_ref, o_ref, m_ref, acc_ref, s_ref, *, tq, heads):
    qi = pl.program_id(2)
    Dv = V_HEAD_DIM
    m_ref[...] = jnp.full_like(m_ref, NEG_INF)
    acc_ref[...] = jnp.zeros_like(acc_ref)

    def scores(j):
        ks = pl.multiple_of(j * tq, tq)
        return [_dot_nt(k_ref[pl.ds(ks, tq), g * QPAD:(g + 1) * QPAD], q_ref[:, g * QPAD:(g + 1) * QPAD])
                for g in range(heads)]

    def softmax_pv(j, sts, masked):
        for g in range(heads):
            st = sts[g]
            if masked:
                kc = lax.broadcasted_iota(I32, (tq, tq), 0) // CHUNK
                qc = lax.broadcasted_iota(I32, (tq, tq), 1) // CHUNK
                st = jnp.where(kc <= qc, st, NEG_INF)
            m_prev = m_ref[g]
            m_new = jnp.maximum(m_prev, jnp.max(st, axis=0, keepdims=True))
            alpha = jnp.exp2(m_prev - m_new)
            p = jnp.exp2(st - m_new)
            acc_ref[g] = alpha * acc_ref[g] + _dot(vt_ref[j, g * VT_ROWS:(g + 1) * VT_ROWS, :], p.astype(BF16))
            m_ref[g] = m_new

    for g, st in enumerate(scores(0)):
        s_ref[g] = st

    def body(j, c):
        cur = [s_ref[g] for g in range(heads)]
        nxt = scores(j + 1)
        softmax_pv(j, cur, False)
        for g in range(heads):
            s_ref[g] = nxt[g]
        return c

    lax.fori_loop(0, qi, body, 0)
    softmax_pv(qi, [s_ref[g] for g in range(heads)], True)
    for g in range(heads):
        out_t = acc_ref[g, :Dv, :] / acc_ref[g, Dv:Dv + 1, :]
        o_ref[:, g * Dv:(g + 1) * Dv] = out_t.T.astype(o_ref.dtype)


def _mla_attention(q_cat, k_cat, v_t, B, S, tq):
    H = MLA_HEADS
    T = q_cat.shape[0]
    G = 8
    nq = S // tq
    once = pl.Buffered(1)
    return pl.pallas_call(
        functools.partial(_mla_attn_kernel, tq=tq, heads=G),
        grid=(B, H // G, nq),
        in_specs=[pl.BlockSpec((tq, G * QPAD), lambda b, h, i: (b * nq + i, h)),
                  pl.BlockSpec((S, G * QPAD), lambda b, h, i: (b, h), pipeline_mode=once),
                  pl.BlockSpec((nq, G * VT_ROWS, tq), lambda b, h, i: (b, h, 0), pipeline_mode=once)],
        out_specs=pl.BlockSpec((tq, G * V_HEAD_DIM), lambda b, h, i: (b * nq + i, h)),
        out_shape=jax.ShapeDtypeStruct((T, H * V_HEAD_DIM), BF16),
        scratch_shapes=[pltpu.VMEM((G, 1, tq), F32),
                        pltpu.VMEM((G, VT_ROWS, tq), F32), pltpu.VMEM((G, tq, tq), F32)],
        compiler_params=_cparams("parallel", "parallel", "arbitrary"),
        name="mla_attention",
    )(q_cat, k_cat, v_t)


def _rot_cols(w):
    half = QK_ROPE_DIM // 2
    return jnp.concatenate([-w[..., half:], w[..., :half]], axis=-1)


def _q_weights(w_uq):
    R = w_uq.shape[0]
    w = w_uq.reshape(R, MLA_HEADS, QK_NOPE_DIM + QK_ROPE_DIM)
    wn, wr = w[..., :QK_NOPE_DIM], w[..., QK_NOPE_DIM:]
    return jnp.concatenate([wn, wr, _rot_cols(wr)], axis=-1).transpose(1, 0, 2).astype(BF16)


def kernel(x, c, positions, w_mod, b_mod, g_mix, g_ffn, w_qkv_a, w_o_a, g_kv_in, w_mod_kv, b_mod_kv,
           w_dkv, g_ckv, w_uk, w_uv, w_dq_b, g_cq_b, w_uq_b, w_o_b, router_w, router_b, w_gu, b_gu,
           w_down, b_down, g_final):
    B, S, D = x.shape
    T = B * S
    depth = w_mod.shape[0]
    n_a = w_qkv_a.shape[0]
    xt = x.reshape(T, D)

    mod = _adaln(c, w_mod, b_mod)
    mod_kv = _adaln(c, w_mod_kv[None], b_mod_kv[None])[0]

    half = QK_ROPE_DIM // 2
    inv_freq = ROPE_BASE ** (-jnp.arange(half, dtype=F32) / half)
    ang = positions.astype(F32).reshape(T, 1) * inv_freq
    cos_f, sin_f = _cossin(ang.reshape(T * half // LANES, LANES))
    cos = cos_f.reshape(T, half)
    sin = sin_f.reshape(T, half)
    z64 = jnp.zeros((T, QK_ROPE_DIM), F32)
    cos_k = jnp.concatenate([cos, cos, z64], axis=1)
    sin_k = jnp.concatenate([sin, sin, z64], axis=1)
    qs = LOG2_E * (QK_NOPE_DIM + QK_ROPE_DIM) ** -0.5
    cos_q = jnp.concatenate([jnp.ones((T, QK_NOPE_DIM), F32), cos_k], axis=1) * qs
    sin_q = jnp.concatenate([jnp.zeros((T, QK_NOPE_DIM), F32), sin_k], axis=1) * qs

    tq_mla = min(S, 256)
    w_gu_b = w_gu.astype(BF16)
    w_down_b = w_down.astype(BF16)

    shared = None
    for layer in range(depth):
        m = mod[layer]
        shift_a, scale_a, gate_a, shift_m, scale_m, gate_m = [m[:, i * D:(i + 1) * D] for i in range(6)]
        if layer < n_a:
            qkv = _nm_matmul(xt, g_mix[layer], shift_a, scale_a, w_qkv_a[layer].astype(BF16), S,
                             q_cols=SB_HEADS * SB_HEAD_DIM, q_scale=LOG2_E * SB_HEAD_DIM ** -0.5)
            o = _sb_attention(qkv, B, S)
            w_o = w_o_a[layer]
        else:
            i = layer - n_a
            k_cat, v_b = shared
            cq = _dq(xt, g_mix[layer], shift_a, scale_a, w_dq_b[i].astype(BF16), g_cq_b[i], S)
            q_cat = _uq_rope(cq, _q_weights(w_uq_b[i]), cos_q, sin_q)
            o = _mla_attention(q_cat, k_cat, v_b, B, S, tq_mla)
            w_o = w_o_b[i]
        xt, h, logits_t = _oproj(o, w_o.astype(BF16), xt, gate_a, g_ffn[layer], shift_m, scale_m,
                                 router_w[layer], router_b[layer], S)
        xt = _moe(h, logits_t, xt, gate_m, g_final, w_gu_b, b_gu, w_down_b, b_down, layer, S,
                  final=(layer == depth - 1))
        if layer == n_a - 1:
            w_ext = jnp.concatenate([w_dkv, _rot_cols(w_dkv[:, KV_LORA_RANK:])], axis=1).astype(BF16)
            ckv, k_rope = _kv_latent(xt, g_kv_in, mod_kv[:, :D], mod_kv[:, D:], w_ext, g_ckv,
                                     cos_k, sin_k, S)
            shared = _kv_up(ckv, w_uk.astype(BF16), w_uv.T.astype(BF16), k_rope, tq_mla)
    return xt.reshape(B, S, D)
```
